```python
import math
import jax, jax.numpy as jnp
from jax import lax
import numpy as np

D_MODEL = 1024
BATCH = 32
SEQ = 2048
DEPTH = 4
DEC_BATCH = 32
DEC_SEQ = 64
PAST_LEN = 2048

CHUNK = 64
N_MIXERS = 2
N_A = (DEPTH + 1) // 2
N_B = DEPTH // 2
HEAD_DIM = 64
N_Q_HEADS = D_MODEL // HEAD_DIM
N_KV_HEADS = 4
Q_PER_KV = N_Q_HEADS // N_KV_HEADS
WINDOW = 128
N_BACK = -(-WINDOW // CHUNK)
WINDOW_ROWS = N_BACK * CHUNK
ROPE_THETA = 10000.0
HEAD_B = 64
N_HEADS_B = D_MODEL // HEAD_B
LORA_DECAY = 64
LORA_A = 64
LORA_V = 32
LORA_G = 160
GN_EPS = 64e-5
N_GROUPS = 4
E_PER_GROUP = 8
N_EXPERTS = N_GROUPS * E_PER_GROUP
TOP_K = 2
D_EXPERT = 512
MOE_BLOCK = 128
LN_EPS = 1e-5
ALPHA = (2 * DEPTH) ** 0.25
BETA = (8 * DEPTH) ** -0.25

kernel_name = 'hybrid_swa_sink_rwkv7_hmoe_stream_step'


def _layer_norm(x, g, b):
    xf = x.astype(jnp.float32)
    mu = jnp.mean(xf, axis=-1, keepdims=True)
    var = jnp.mean(jnp.square(xf - mu), axis=-1, keepdims=True)
    return ((xf - mu) * lax.rsqrt(var + LN_EPS) * g + b).astype(x.dtype)


def _rope(x, pos):
    inv = ROPE_THETA ** (-(jnp.arange(0, HEAD_DIM, 2, dtype=jnp.float32) / HEAD_DIM))
    ang = pos.astype(jnp.float32)[:, None] * inv[None, :]
    cos = jnp.cos(ang)[:, None, :]
    sin = jnp.sin(ang)[:, None, :]
    xf = x.astype(jnp.float32)
    x1, x2 = xf[..., :HEAD_DIM // 2], xf[..., HEAD_DIM // 2:]
    return jnp.concatenate([x1 * cos - x2 * sin, x2 * cos + x1 * sin], axis=-1).astype(x.dtype)


def _qkv(x, w_qkv, pos):
    b, s, _ = x.shape
    qkv = x @ w_qkv
    nq, nk = N_Q_HEADS * HEAD_DIM, N_KV_HEADS * HEAD_DIM
    q = qkv[..., :nq].reshape(b, s, N_Q_HEADS, HEAD_DIM)
    k = qkv[..., nq:nq + nk].reshape(b, s, N_KV_HEADS, HEAD_DIM)
    v = qkv[..., nq + nk:].reshape(b, s, N_KV_HEADS, HEAD_DIM)
    return _rope(q, pos), _rope(k, pos), v


def _sink_attn(q, k, v, valid, sinks):
    s = jnp.einsum('bcqhgd,bckhd->bchgqk', q, k).astype(jnp.float32) * (HEAD_DIM ** -0.5)
    s = jnp.where(valid[None, :, None, None, None, :], s, -jnp.inf)
    sink = sinks.astype(jnp.float32)[None, None, :, :, None, None]
    m = jnp.maximum(jnp.max(s, axis=-1, keepdims=True), sink)
    p = jnp.exp(s - m)
    p = p / (jnp.sum(p, axis=-1, keepdims=True) + jnp.exp(sink - m))
    return jnp.einsum('bchgqk,bckhd->bcqhgd', p.astype(v.dtype), v)


def _attn_prompt(x, w_qkv, sinks, w_o):
    b, s, _ = x.shape
    nc = s // CHUNK
    q, k, v = _qkv(x, w_qkv, jnp.arange(s))
    pad = ((0, 0), (WINDOW_ROWS, 0), (0, 0), (0, 0))
    kp = jnp.pad(k, pad).reshape(b, nc + N_BACK, CHUNK, N_KV_HEADS, HEAD_DIM)
    vp = jnp.pad(v, pad).reshape(b, nc + N_BACK, CHUNK, N_KV_HEADS, HEAD_DIM)
    kb = jnp.concatenate([kp[:, j:j + nc] for j in range(N_BACK + 1)], axis=2)
    vb = jnp.concatenate([vp[:, j:j + nc] for j in range(N_BACK + 1)], axis=2)
    kpos = (jnp.arange(nc)[:, None] - N_BACK) * CHUNK + jnp.arange((N_BACK + 1) * CHUNK)[None, :]
    o = _sink_attn(q.reshape(b, nc, CHUNK, N_KV_HEADS, Q_PER_KV, HEAD_DIM), kb, vb, kpos >= 0,
                   sinks.reshape(N_KV_HEADS, Q_PER_KV))
    keep = min(WINDOW_ROWS, s)
    return o.reshape(b, s, N_Q_HEADS * HEAD_DIM) @ w_o, k[:, s - keep:], v[:, s - keep:]


def _attn_sample(x, ck, cv, w_qkv, sinks, w_o):
    b, t, _ = x.shape
    q, k, v = _qkv(x, w_qkv, PAST_LEN + jnp.arange(t))
    kb = jnp.concatenate([ck.astype(k.dtype), k], axis=1)[:, None]
    vb = jnp.concatenate([cv.astype(v.dtype), v], axis=1)[:, None]
    valid = jnp.ones((1, kb.shape[2]), dtype=bool)
    o = _sink_attn(q.reshape(b, 1, t, N_KV_HEADS, Q_PER_KV, HEAD_DIM), kb, vb, valid,
                   sinks.reshape(N_KV_HEADS, Q_PER_KV))
    return o.reshape(b, t, N_Q_HEADS * HEAD_DIM) @ w_o, k, v


def _wkv_step(s, inp):
    r, w, k, v, a, bb = inp
    sa = jnp.einsum('bhvk,bhk->bhv', s, a)
    s = s * w[:, :, None, :] + sa[..., None] * bb[:, :, None, :] + v[..., None] * k[:, :, None, :]
    return s, jnp.einsum('bhvk,bhk->bhv', s, r)


def _rwkv7(x, x_last, s0, v_first, vres, mu, w_rkv, w0, w1, w2, a0, a1, a2, g1, g2,
           k_k, k_a, r_k, lnx_g, lnx_b, w_o):
    b, t, d = x.shape
    xx = jnp.concatenate([x_last[:, None].astype(x.dtype), x[:, :-1]], axis=1) - x
    mix = lambda j: x + xx * mu[j]
    r = mix(0) @ w_rkv[0]
    wlog = -jax.nn.softplus(-(w0 + jnp.tanh(mix(1) @ w1) @ w2)) - 0.5
    k = mix(2) @ w_rkv[1]
    xv = mix(3)
    v = xv @ w_rkv[2]
    if vres is None:
        v_first = v
    else:
        v0, v1, v2 = vres
        v = v + (v_first - v) * jax.nn.sigmoid(v0 + (xv @ v1) @ v2)
    a = jax.nn.sigmoid(a0 + (mix(4) @ a1) @ a2)
    g = jax.nn.sigmoid(mix(5) @ g1) @ g2
    heads = lambda z: z.reshape(b, t, N_HEADS_B, HEAD_B).astype(jnp.float32)
    kk = heads(k * k_k)
    kk = kk / jnp.maximum(jnp.linalg.norm(kk, axis=-1, keepdims=True), 1e-12)
    k = k * (1 + (a - 1) * k_a)
    rh, kh, vh, ah = heads(r), heads(k), heads(v), heads(a)
    decay = jnp.exp(-jnp.exp(heads(wlog)))
    tm = lambda z: jnp.moveaxis(z, 1, 0)
    s_fin, yt = lax.scan(_wkv_step, s0.astype(jnp.float32),
                         (tm(rh), tm(decay), tm(kh), tm(vh), tm(-kk), tm(kk * ah)))
    y = jnp.moveaxis(yt, 0, 1)
    ym = jnp.mean(y, axis=-1, keepdims=True)
    yv = jnp.mean(jnp.square(y - ym), axis=-1, keepdims=True)
    y = ((y - ym) * lax.rsqrt(yv + GN_EPS)).reshape(b, t, d) * lnx_g + lnx_b
    y = y + (jnp.sum(rh * kh * r_k, axis=-1, keepdims=True) * vh).reshape(b, t, d)
    return (y.astype(x.dtype) * g) @ w_o, x[:, -1], s_fin, v_first


def _hier_moe(x, w_rg, b_rg, w_re, b_re, w_in, w_out):
    b, t, d = x.shape
    n = b * t
    x2 = x.reshape(n, d)
    lg = (x2 @ w_rg).astype(jnp.float32) + b_rg.astype(jnp.float32)
    grp = jnp.argmax(lg, axis=-1)
    pg = jnp.take_along_axis(jax.nn.softmax(lg, axis=-1), grp[:, None], axis=1)
    le = ((x2 @ w_re).astype(jnp.float32) + b_re.astype(jnp.float32)).reshape(n, N_GROUPS, E_PER_GROUP)
    le = jnp.take_along_axis(le, grp[:, None, None], axis=1)[:, 0]
    top_l, top_i = lax.top_k(le, TOP_K)
    gate = (pg * jax.nn.softmax(top_l, axis=-1)).reshape(-1)
    eid = (grp[:, None] * E_PER_GROUP + top_i).reshape(-1).astype(jnp.int32)
    tok = jnp.repeat(jnp.arange(n, dtype=jnp.int32), TOP_K)
    na = n * TOP_K
    nb = -(-na // MOE_BLOCK) + N_EXPERTS
    order = jnp.argsort(eid)
    e_sorted = eid[order]
    counts = jnp.zeros((N_EXPERTS,), jnp.int32).at[eid].add(1)
    padded = (counts + MOE_BLOCK - 1) // MOE_BLOCK * MOE_BLOCK
    pad_end = jnp.cumsum(padded)
    start = jnp.cumsum(counts) - counts
    dest = pad_end[e_sorted] - padded[e_sorted] + jnp.arange(na, dtype=jnp.int32) - start[e_sorted]
    buf_tok = jnp.full((nb * MOE_BLOCK,), n, jnp.int32).at[dest].set(tok[order])
    buf_gate = jnp.zeros((nb * MOE_BLOCK,), jnp.float32).at[dest].set(gate[order])
    blk_e = jnp.minimum(jnp.searchsorted(pad_end, jnp.arange(nb, dtype=jnp.int32) * MOE_BLOCK, side='right'),
                        N_EXPERTS - 1)
    x_pad = jnp.concatenate([x2, jnp.zeros((1, d), x2.dtype)], axis=0)

    def expert_block(args):
        tok_b, gate_b, e = args
        h = x_pad[tok_b] @ w_in[e]
        h = jax.nn.silu(h[:, :D_EXPERT]) * h[:, D_EXPERT:]
        return ((h @ w_out[e]) * gate_b[:, None]).astype(x2.dtype)

    yb = lax.map(expert_block, (buf_tok.reshape(nb, MOE_BLOCK), buf_gate.reshape(nb, MOE_BLOCK), blk_e))
    y = jnp.zeros((n + 1, d), x2.dtype).at[buf_tok].add(yb.reshape(-1, d))
    return y[:n].reshape(b, t, d)


def setup_inputs(seed: int = 0) -> dict:
    key = jax.random.key(seed)
    ks = iter(jax.random.split(key, 48))
    nrm = lambda shape, scale: jax.random.normal(next(ks), shape, jnp.float32) * scale
    unif = lambda shape, lo, hi: jax.random.uniform(next(ks), shape, jnp.float32, lo, hi)
    keep = min(WINDOW_ROWS, PAST_LEN)
    d = D_MODEL
    qkv_w = (N_Q_HEADS + 2 * N_KV_HEADS) * HEAD_DIM
    nv = max(N_B - 1, 0)
    return {
        'x_prompt': nrm((BATCH, SEQ, d), 1.0),
        'x_sample': nrm((DEC_BATCH, DEC_SEQ, d), 1.0),
        'cache_k_a': nrm((N_A, DEC_BATCH, keep, N_KV_HEADS, HEAD_DIM), 1.0),
        'cache_v_a': nrm((N_A, DEC_BATCH, keep, N_KV_HEADS, HEAD_DIM), 1.0),
        'state_shift_b': nrm((N_B, DEC_BATCH, d), 1.0),
        'state_wkv_b': nrm((N_B, DEC_BATCH, N_HEADS_B, HEAD_B, HEAD_B), 0.5),
        'w_qkv_a': nrm((N_A, d, qkv_w), d ** -0.5),
        'sinks_a': nrm((N_A, N_Q_HEADS), 0.5),
        'w_o_a': nrm((N_A, N_Q_HEADS * HEAD_DIM, d), (N_Q_HEADS * HEAD_DIM) ** -0.5 * BETA),
        'mu_b': unif((N_B, 6, d), 0.0, 1.0),
        'w_rkv_b': nrm((N_B, 3, d, d), d ** -0.5),
        'w0_b': unif((N_B, d), -4.0, 0.0),
        'w1_b': nrm((N_B, d, LORA_DECAY), d ** -0.5),
        'w2_b': nrm((N_B, LORA_DECAY, d), 0.1 * LORA_DECAY ** -0.5),
        'a0_b': nrm((N_B, d), 0.1),
        'a1_b': nrm((N_B, d, LORA_A), d ** -0.5),
        'a2_b': nrm((N_B, LORA_A, d), LORA_A ** -0.5),
        'v0_b': nrm((nv, d), 0.1),
        'v1_b': nrm((nv, d, LORA_V), d ** -0.5),
        'v2_b': nrm((nv, LORA_V, d), LORA_V ** -0.5),
        'g1_b': nrm((N_B, d, LORA_G), d ** -0.5),
        'g2_b': nrm((N_B, LORA_G, d), LORA_G ** -0.5),
        'k_k_b': 0.85 + nrm((N_B, d), 0.05),
        'k_a_b': 1.0 + nrm((N_B, d), 0.05),
        'r_k_b': nrm((N_B, N_HEADS_B, HEAD_B), 0.1),
        'lnx_g_b': 1.0 + nrm((N_B, d), 0.05),
        'lnx_b_b': nrm((N_B, d), 0.02),
        'w_o_b': nrm((N_B, d, d), d ** -0.5 * BETA),
        'ln_g': 1.0 + nrm((DEPTH, 2, d), 0.05),
        'ln_b': nrm((DEPTH, 2, d), 0.02),
        'w_rg': nrm((DEPTH, d, N_GROUPS), d ** -0.5),
        'b_rg': nrm((DEPTH, N_GROUPS), 0.01),
        'w_re': nrm((DEPTH, d, N_EXPERTS), d ** -0.5),
        'b_re': nrm((DEPTH, N_EXPERTS), 0.01),
        'w_exp_in': nrm((DEPTH, N_EXPERTS, d, 2 * D_EXPERT), d ** -0.5),
        'w_exp_out': nrm((DEPTH, N_EXPERTS, D_EXPERT, d), D_EXPERT ** -0.5 * BETA),
    }


def reference(x_prompt, x_sample, cache_k_a, cache_v_a, state_shift_b, state_wkv_b,
              w_qkv_a, sinks_a, w_o_a, mu_b, w_rkv_b, w0_b, w1_b, w2_b, a0_b, a1_b, a2_b,
              v0_b, v1_b, v2_b, g1_b, g2_b, k_k_b, k_a_b, r_k_b, lnx_g_b, lnx_b_b, w_o_b,
              ln_g, ln_b, w_rg, b_rg, w_re, b_re, w_exp_in, w_exp_out):
    xp, xs = x_prompt, x_sample
    kp_l, vp_l, ks_l, vs_l = [], [], [], []
    shp_l, stp_l, shs_l, sts_l = [], [], [], []
    vf_p = None
    vf_s = None
    for i in range(DEPTH):
        j = i // N_MIXERS
        if i % N_MIXERS == 0:
            hp, kn, vn = _attn_prompt(xp, w_qkv_a[j], sinks_a[j], w_o_a[j])
            hs, kns, vns = _attn_sample(xs, cache_k_a[j], cache_v_a[j], w_qkv_a[j], sinks_a[j], w_o_a[j])
            kp_l.append(kn); vp_l.append(vn); ks_l.append(kns); vs_l.append(vns)
        else:
            vres = None if j == 0 else (v0_b[j - 1], v1_b[j - 1], v2_b[j - 1])
            prm = (mu_b[j], w_rkv_b[j], w0_b[j], w1_b[j], w2_b[j], a0_b[j], a1_b[j], a2_b[j],
                   g1_b[j], g2_b[j], k_k_b[j], k_a_b[j], r_k_b[j], lnx_g_b[j], lnx_b_b[j], w_o_b[j])
            s0p = jnp.zeros((xp.shape[0], N_HEADS_B, HEAD_B, HEAD_B), jnp.float32)
            hp, shp, stp, vf_p = _rwkv7(xp, jnp.zeros_like(xp[:, 0]), s0p, vf_p, vres, *prm)
            hs, shs, sts, vf_s = _rwkv7(xs, state_shift_b[j], state_wkv_b[j], vf_s, vres, *prm)
            shp_l.append(shp); stp_l.append(stp); shs_l.append(shs); sts_l.append(sts)
        xp = _layer_norm(ALPHA * xp + hp, ln_g[i, 0], ln_b[i, 0])
        xs = _layer_norm(ALPHA * xs + hs, ln_g[i, 0], ln_b[i, 0])
        moe_w = (w_rg[i], b_rg[i], w_re[i], b_re[i], w_exp_in[i], w_exp_out[i])
        xp = _layer_norm(ALPHA * xp + _hier_moe(xp, *moe_w), ln_g[i, 1], ln_b[i, 1])
        xs = _layer_norm(ALPHA * xs + _hier_moe(xs, *moe_w), ln_g[i, 1], ln_b[i, 1])
    return (xp, xs,
            jnp.stack(kp_l), jnp.stack(vp_l), jnp.stack(shp_l), jnp.stack(stp_l),
            jnp.stack(ks_l), jnp.stack(vs_l), jnp.stack(shs_l), jnp.stack(sts_l))
```

```python
import functools

import jax
import jax.numpy as jnp
from jax import lax
from jax.experimental import pallas as pl
from jax.experimental.pallas import tpu as pltpu

F32 = jnp.float32
BF16 = jnp.bfloat16

CHUNK = 64
HEAD_DIM = 64
N_Q_HEADS = 16
N_KV_HEADS = 4
WINDOW_ROWS = 128
ROPE_THETA = 10000.0
PAST_LEN = 2048
HEAD_B = 64
GN_EPS = 64e-5
N_GROUPS = 4
E_PER_GROUP = 8
N_EXPERTS = 32
D_EXPERT = 512
LN_EPS = 1e-5
DEPTH = 4
ALPHA = (2 * DEPTH) ** 0.25

LANES = 128
VMEM_LIMIT = 56 * 1024 * 1024
NEG_BIG = -1e30


def _cparams(sem):
    return pltpu.CompilerParams(dimension_semantics=sem, vmem_limit_bytes=VMEM_LIMIT)


def _pick_tile(cands, divides=(), multiple_of=()):
    for t in cands:
        if all(n % t == 0 for n in divides) and all(t % m == 0 for m in multiple_of):
            return t
    raise ValueError(f"no tile in {cands} for {divides} / {multiple_of}")


def _layer_norm(z, g, b):
    mu = jnp.mean(z, axis=-1, keepdims=True)
    zc = z - mu
    var = jnp.mean(zc * zc, axis=-1, keepdims=True)
    return zc * lax.rsqrt(var + LN_EPS) * g + b


def _dot(a, b):
    return jnp.dot(a, b, preferred_element_type=F32)


def _sigmoid(z):
    return 1.0 / (1.0 + jnp.exp(-z))


def _qkv_kernel(x_ref, w_ref, cos_ref, sin_ref, q_ref, k_ref, v_ref):
    nq = q_ref.shape[1]
    nk = k_ref.shape[1]
    acc = _dot(x_ref[...].astype(BF16), w_ref[...])
    cos = cos_ref[...]
    sin = sin_ref[...]
    lane = lax.broadcasted_iota(jnp.int32, cos.shape, 1)
    first_half = (lane & (HEAD_DIM - 1)) < HEAD_DIM // 2

    def rope(xg):
        rot = jnp.where(first_half, pltpu.roll(xg, LANES - HEAD_DIM // 2, 1),
                        pltpu.roll(xg, HEAD_DIM // 2, 1))
        return xg * cos + rot * sin

    for g in range(nq // LANES):
        sl = slice(g * LANES, (g + 1) * LANES)
        q_ref[:, sl] = rope(acc[:, sl]).astype(q_ref.dtype)
    for g in range(nk // LANES):
        k_ref[:, g * LANES:(g + 1) * LANES] = rope(acc[:, nq + g * LANES:nq + (g + 1) * LANES])
    v_ref[...] = acc[:, nq + nk:]


def _qkv_rope(x, w_bf, cos_t, sin_t, tm, n_prompt, seq):
    n, d = x.shape
    nq, nk = N_Q_HEADS * HEAD_DIM, N_KV_HEADS * HEAD_DIM
    tiles_per_seq = seq // tm
    prompt_tiles = n_prompt // tm

    def tab_map(i):
        return (jnp.where(i < prompt_tiles, i % tiles_per_seq, tiles_per_seq), 0)

    return pl.pallas_call(
        _qkv_kernel,
        grid=(n // tm,),
        in_specs=[
            pl.BlockSpec((tm, d), lambda i: (i, 0)),
            pl.BlockSpec((d, nq + 2 * nk), lambda i: (0, 0)),
            pl.BlockSpec((tm, LANES), tab_map),
            pl.BlockSpec((tm, LANES), tab_map),
        ],
        out_specs=[
            pl.BlockSpec((tm, nq), lambda i: (i, 0)),
            pl.BlockSpec((tm, nk), lambda i: (i, 0)),
            pl.BlockSpec((tm, nk), lambda i: (i, 0)),
        ],
        out_shape=[
            jax.ShapeDtypeStruct((n, nq), BF16),
            jax.ShapeDtypeStruct((n, nk), F32),
            jax.ShapeDtypeStruct((n, nk), F32),
        ],
        compiler_params=_cparams(("parallel",)),
        name="qkv_rope",
    )(x, w_bf, cos_t, sin_t)


def _rope_tables(seq, dec_seq, tm):
    inv = ROPE_THETA ** (-(jnp.arange(0, HEAD_DIM, 2, dtype=F32) / HEAD_DIM))
    pos = jnp.concatenate([jnp.arange(seq), PAST_LEN + (jnp.arange(tm) % dec_seq)]).astype(F32)
    ang = pos[:, None] * inv[None, :]
    cos, sin = jnp.cos(ang), jnp.sin(ang)
    reps = LANES // HEAD_DIM
    cos_t = jnp.tile(jnp.concatenate([cos, cos], axis=1), (1, reps))
    sin_t = jnp.tile(jnp.concatenate([-sin, sin], axis=1), (1, reps))
    return cos_t, sin_t


def _attn_kernel(sinks_ref, q_ref, kh_ref, km_ref, vh_ref, vm_ref, o_ref, *, qb, mask_halo):
    blk = pl.program_id(1)
    scale = HEAD_DIM ** -0.5
    kall = jnp.concatenate([kh_ref[0], km_ref[0]], axis=0)
    vall = jnp.concatenate([vh_ref[0], vm_ref[0]], axis=0)
    rows = kall.shape[0]
    lane = lax.broadcasted_iota(jnp.int32, (rows, LANES), 1)
    low = lane < HEAD_DIM
    band = WINDOW_ROWS + CHUNK
    key_row = lax.broadcasted_iota(jnp.int32, (CHUNK, band), 1)

    variants = []
    for kvh in range(N_KV_HEADS):
        sl = slice(LANES * (kvh // 2), LANES * (kvh // 2 + 1))
        own = low if kvh % 2 == 0 else jnp.logical_not(low)
        per = []
        for arr in (kall, vall):
            a_own = jnp.where(own, arr[:, sl], 0.0)
            a_oth = pltpu.roll(a_own, HEAD_DIM, 1)
            lo, hi = (a_own, a_oth) if kvh % 2 == 0 else (a_oth, a_own)
            per.append((lo.astype(BF16), hi.astype(BF16)))
        variants.append(per)

    for c in range(qb // CHUNK):
        r0 = c * CHUNK
        if mask_halo and r0 < WINDOW_ROWS:
            valid = key_row >= jnp.where(blk > 0, 0, WINDOW_ROWS - r0)
        else:
            valid = None
        for kvh in range(N_KV_HEADS):
            (k_lo, k_hi), (v_lo, v_hi) = variants[kvh]
            for pair in range(2):
                col = LANES * (2 * kvh + pair)
                q2 = q_ref[0, r0:r0 + CHUNK, col:col + LANES]
                out = None
                for half, (kk_, vv_) in enumerate(((k_lo, v_lo), (k_hi, v_hi))):
                    head = 4 * kvh + 2 * pair + half
                    sink = sinks_ref[head]
                    s = lax.dot_general(q2, kk_[r0:r0 + band], (((1,), (1,)), ((), ())),
                                        preferred_element_type=F32) * scale
                    if valid is not None:
                        s = jnp.where(valid, s, NEG_BIG)
                    m = jnp.maximum(jnp.max(s, axis=-1, keepdims=True), sink)
                    p = jnp.exp(s - m)
                    p = p / (jnp.sum(p, axis=-1, keepdims=True) + jnp.exp(sink - m))
                    pv = _dot(p.astype(BF16), vv_[r0:r0 + band])
                    out = pv if out is None else out + pv
                o_ref[0, r0:r0 + CHUNK, col:col + LANES] = out.astype(o_ref.dtype)


def _attention(q, k_halo, k_main, v_halo, v_main, sinks, qb, mask_halo, halo_from_main):
    b, s, nq = q.shape
    nk = k_main.shape[2]
    hb = qb // WINDOW_ROWS if halo_from_main else None

    def halo_map(bi, j):
        if halo_from_main:
            return (bi, jnp.maximum(j * hb - 1, 0), 0)
        return (bi, 0, 0)

    return pl.pallas_call(
        functools.partial(_attn_kernel, qb=qb, mask_halo=mask_halo),
        grid=(b, s // qb),
        in_specs=[
            pl.BlockSpec(memory_space=pltpu.SMEM),
            pl.BlockSpec((1, qb, nq), lambda bi, j: (bi, j, 0)),
            pl.BlockSpec((1, WINDOW_ROWS, nk), halo_map),
            pl.BlockSpec((1, qb, nk), lambda bi, j: (bi, j, 0)),
            pl.BlockSpec((1, WINDOW_ROWS, nk), halo_map),
            pl.BlockSpec((1, qb, nk), lambda bi, j: (bi, j, 0)),
        ],
        out_specs=pl.BlockSpec((1, qb, nq), lambda bi, j: (bi, j, 0)),
        out_shape=jax.ShapeDtypeStruct((b, s, nq), BF16),
        compiler_params=_cparams(("parallel", "parallel")),
        name="attn_prompt" if mask_halo else "attn_sample",
    )(sinks, q, k_halo, k_main, v_halo, v_main)


def _proj_ln_kernel(a_ref, w_ref, x_ref, g_ref, b_ref, o_ref):
    h = _dot(a_ref[...], w_ref[...])
    o_ref[...] = _layer_norm(ALPHA * x_ref[...] + h, g_ref[...], b_ref[...])


def _proj_ln(a, w_bf, x, g, b, tm):
    n, d = x.shape
    ka = a.shape[1]
    return pl.pallas_call(
        _proj_ln_kernel,
        grid=(n // tm,),
        in_specs=[
            pl.BlockSpec((tm, ka), lambda i: (i, 0)),
            pl.BlockSpec((ka, d), lambda i: (0, 0)),
            pl.BlockSpec((tm, d), lambda i: (i, 0)),
            pl.BlockSpec((1, d), lambda i: (0, 0)),
            pl.BlockSpec((1, d), lambda i: (0, 0)),
        ],
        out_specs=pl.BlockSpec((tm, d), lambda i: (i, 0)),
        out_shape=jax.ShapeDtypeStruct((n, d), F32),
        compiler_params=_cparams(("parallel",)),
        name="proj_ln",
    )(a, w_bf, x, g, b)


GROUP_LANE0 = N_EXPERTS


def _router_kernel(x_ref, w_ref, b_ref, o_ref):
    lg = _dot(x_ref[...].astype(BF16), w_ref[...]) + b_ref[...]
    lane = lax.broadcasted_iota(jnp.int32, lg.shape, 1)
    lane_f = lane.astype(F32)
    is_grp = jnp.logical_and(lane >= GROUP_LANE0, lane < GROUP_LANE0 + N_GROUPS)

    def top1(mask):
        m = jnp.max(jnp.where(mask, lg, NEG_BIG), axis=-1, keepdims=True)
        idx = jnp.min(jnp.where(jnp.logical_and(mask, lg == m), lane_f, 1e9), axis=-1, keepdims=True)
        return m, idx

    gm, gidx = top1(is_grp)
    grp = gidx - GROUP_LANE0
    pg = 1.0 / jnp.sum(jnp.where(is_grp, jnp.exp(lg - gm), 0.0), axis=-1, keepdims=True)
    in_grp = jnp.logical_and(lane < N_EXPERTS, (lane >> 3).astype(F32) == grp)
    m1, i1 = top1(in_grp)
    m2, i2 = top1(jnp.logical_and(in_grp, lane_f != i1))
    t = jnp.exp(m2 - m1)
    s1 = 1.0 / (1.0 + t)
    s2 = t / (1.0 + t)
    out = jnp.where(lane == 0, i1, jnp.where(lane == 1, i2,
          jnp.where(lane == 2, pg * s1, jnp.where(lane == 3, pg * s2, 0.0))))
    o_ref[...] = out


def _router(x, w_bf, bias, tm):
    n, d = x.shape
    return pl.pallas_call(
        _router_kernel,
        grid=(n // tm,),
        in_specs=[
            pl.BlockSpec((tm, d), lambda i: (i, 0)),
            pl.BlockSpec((d, LANES), lambda i: (0, 0)),
            pl.BlockSpec((1, LANES), lambda i: (0, 0)),
        ],
        out_specs=pl.BlockSpec((tm, LANES), lambda i: (i, 0)),
        out_shape=jax.ShapeDtypeStruct((n, LANES), F32),
        compiler_params=_cparams(("parallel",)),
        name="moe_router",
    )(x, w_bf, bias)


def _row_copy(src_hbm, dst, sem, src_row, dst_row):
    return pltpu.make_async_copy(src_hbm.at[pl.ds(src_row, 1)], dst.at[pl.ds(dst_row, 1)], sem)


def _gather_rows(idx_ref, src_hbm, dst, sem, n_rows):
    def body(r, carry):
        _row_copy(src_hbm, dst, sem, idx_ref[0, 0, r], r).start()
        return carry
    lax.fori_loop(0, n_rows, body, 0, unroll=8)


def _wait_rows(src_hbm, dst, sem, n_rows):
    def body(r, carry):
        _row_copy(src_hbm, dst, sem, 0, r).wait()
        return carry
    lax.fori_loop(0, n_rows, body, 0, unroll=8)


def _expert_kernel(blk_e_ref, nvalid_ref, tok_ref, tok_next_ref, x_hbm, win_ref, wout_ref,
                   ys_ref, buf, sem):
    i = pl.program_id(0)
    nb = pl.num_programs(0)
    bm = buf.shape[1]
    slot = i % 2
    nvalid = nvalid_ref[0]

    @pl.when(jnp.logical_and(i == 0, nvalid > 0))
    def _():
        _gather_rows(tok_ref, x_hbm, buf.at[0], sem.at[0], bm)

    @pl.when(jnp.logical_and(i + 1 < nb, i + 1 < nvalid))
    def _():
        nslot = (i + 1) % 2
        _gather_rows(tok_next_ref, x_hbm, buf.at[nslot], sem.at[nslot], bm)

    @pl.when(i < nvalid)
    def _():
        _wait_rows(x_hbm, buf.at[slot], sem.at[slot], bm)
        xb = buf[slot].astype(BF16)
        h = _dot(xb, win_ref[0])
        h1 = h[:, :D_EXPERT]
        act = h1 * _sigmoid(h1) * h[:, D_EXPERT:]
        ys_ref[...] = _dot(act.astype(BF16), wout_ref[0])

    @pl.when(i >= nvalid)
    def _():
        ys_ref[...] = jnp.zeros_like(ys_ref)


def _experts(x, row_tok, blk_e, nvalid, w_in_bf, w_out_bf, bm):
    n, d = x.shape
    nb = row_tok.shape[0]
    grid_spec = pltpu.PrefetchScalarGridSpec(
        num_scalar_prefetch=2,
        grid=(nb,),
        in_specs=[
            pl.BlockSpec((1, 1, bm), lambda i, be, nv: (i, 0, 0), memory_space=pltpu.SMEM),
            pl.BlockSpec((1, 1, bm), lambda i, be, nv: (jnp.minimum(i + 1, nb - 1), 0, 0),
                         memory_space=pltpu.SMEM),
            pl.BlockSpec(memory_space=pl.ANY),
            pl.BlockSpec((1, d, 2 * D_EXPERT), lambda i, be, nv: (be[i], 0, 0)),
            pl.BlockSpec((1, D_EXPERT, d), lambda i, be, nv: (be[i], 0, 0)),
        ],
        out_specs=pl.BlockSpec((bm, d), lambda i, be, nv: (i, 0)),
        scratch_shapes=[pltpu.VMEM((2, bm, d), F32), pltpu.SemaphoreType.DMA((2,))],
    )
    return pl.pallas_call(
        _expert_kernel,
        grid_spec=grid_spec,
        out_shape=jax.ShapeDtypeStruct((nb * bm, d), F32),
        compiler_params=_cparams(("arbitrary",)),
        name="moe_experts",
    )(blk_e, nvalid, row_tok, row_tok, x, w_in_bf, w_out_bf)


def _combine_kernel(pos_ref, pos_next_ref, ys_hbm, route_ref, x_ref, g_ref, b_ref, o_ref, buf, sem):
    i = pl.program_id(0)
    nb = pl.num_programs(0)
    tm = x_ref.shape[0]
    slot = i % 2

    @pl.when(i == 0)
    def _():
        _gather_rows(pos_ref, ys_hbm, buf.at[0], sem.at[0], 2 * tm)

    @pl.when(i + 1 < nb)
    def _():
        nslot = (i + 1) % 2
        _gather_rows(pos_next_ref, ys_hbm, buf.at[nslot], sem.at[nslot], 2 * tm)

    _wait_rows(ys_hbm, buf.at[slot], sem.at[slot], 2 * tm)
    route = route_ref[...]
    y = buf[slot, :tm] * route[:, 2:3] + buf[slot, tm:] * route[:, 3:4]
    o_ref[...] = _layer_norm(ALPHA * x_ref[...] + y, g_ref[...], b_ref[...])


def _combine_ln(ys, pos_tab, route, x, g, b, tm):
    n, d = x.shape
    nb = n // tm
    return pl.pallas_call(
        _combine_kernel,
        grid=(nb,),
        in_specs=[
            pl.BlockSpec((1, 1, 2 * tm), lambda i: (i, 0, 0), memory_space=pltpu.SMEM),
            pl.BlockSpec((1, 1, 2 * tm), lambda i: (jnp.minimum(i + 1, nb - 1), 0, 0),
                         memory_space=pltpu.SMEM),
            pl.BlockSpec(memory_space=pl.ANY),
            pl.BlockSpec((tm, LANES), lambda i: (i, 0)),
            pl.BlockSpec((tm, d), lambda i: (i, 0)),
            pl.BlockSpec((1, d), lambda i: (0, 0)),
            pl.BlockSpec((1, d), lambda i: (0, 0)),
        ],
        out_specs=pl.BlockSpec((tm, d), lambda i: (i, 0)),
        out_shape=jax.ShapeDtypeStruct((n, d), F32),
        scratch_shapes=[pltpu.VMEM((2, 2 * tm, d), F32), pltpu.SemaphoreType.DMA((2,))],
        compiler_params=_cparams(("arbitrary",)),
        name="moe_combine_ln",
    )(pos_tab, pos_tab, ys, route, x, g, b)


def _moe_plan(route, bm, tm):
    n = route.shape[0]
    eid = route[:, :2].astype(jnp.int32).reshape(-1)
    na = eid.shape[0]
    onehot = (eid[:, None] == jnp.arange(N_EXPERTS, dtype=jnp.int32)[None, :]).astype(jnp.int32)
    csum = jnp.cumsum(onehot, axis=0)
    rank = jnp.sum(csum * onehot, axis=1) - 1
    counts = csum[-1]
    padded = (counts + bm - 1) // bm * bm
    pad_end = jnp.cumsum(padded)
    pad_off = pad_end - padded
    start = jnp.cumsum(counts) - counts
    dest = pad_off[eid] + rank
    nb = -(-na // bm) + N_EXPERTS
    order = jnp.argsort(eid, stable=True).astype(jnp.int32)
    rows = jnp.arange(nb * bm, dtype=jnp.int32)
    e_row = jnp.minimum(jnp.searchsorted(pad_end, rows, side="right"), N_EXPERTS - 1).astype(jnp.int32)
    local = rows - pad_off[e_row]
    valid = jnp.logical_and(local < counts[e_row], rows < pad_end[-1])
    src = jnp.clip(start[e_row] + local, 0, na - 1)
    row_tok = jnp.where(valid, order[src] // 2, 0).astype(jnp.int32).reshape(nb, 1, bm)
    blk_e = jnp.minimum(jnp.searchsorted(pad_end, jnp.arange(nb, dtype=jnp.int32) * bm, side="right"),
                        N_EXPERTS - 1).astype(jnp.int32)
    nvalid = (pad_end[-1] // bm).astype(jnp.int32).reshape(1)
    pos_tab = dest.astype(jnp.int32).reshape(n // tm, tm, 2).transpose(0, 2, 1).reshape(n // tm, 1, 2 * tm)
    return row_tok, blk_e, nvalid, pos_tab


def _hier_moe_ln(x, w_router_bf, b_router, w_in_bf, w_out_bf, g, b):
    n = x.shape[0]
    tm_r = _pick_tile((512, 256, 128, 64), divides=(n,))
    tm_c = _pick_tile((256, 128, 64), divides=(n,))
    bm = 256 if (2 * n) % 256 == 0 else 128
    route = _router(x, w_router_bf, b_router, tm_r)
    row_tok, blk_e, nvalid, pos_tab = _moe_plan(route, bm, tm_c)
    ys = _experts(x, row_tok, blk_e, nvalid, w_in_bf, w_out_bf, bm)
    return _combine_ln(ys, pos_tab, route, x, g, b, tm_c)


def _head_sum(z, ones_blk):
    hi = z.astype(BF16)
    lo = (z - hi.astype(F32)).astype(BF16)
    return _dot(hi, ones_blk) + _dot(lo, ones_blk)


def _rwkv_proj_kernel(*refs, has_vres):
    if has_vres:
        (x_ref, xp_ref, mu_ref, wr_ref, wk_ref, wv_ref, w0_ref, w1_ref, w2_ref, a0_ref, a1_ref, a2_ref,
         g1_ref, g2_ref, kk_ref, ka_ref, ones_ref, v0_ref, v1_ref, v2_ref, vf_ref,
         r_o, k_o, v_o, kk_o, b_o, lw_o, g_o) = refs
    else:
        (x_ref, xp_ref, mu_ref, wr_ref, wk_ref, wv_ref, w0_ref, w1_ref, w2_ref, a0_ref, a1_ref, a2_ref,
         g1_ref, g2_ref, kk_ref, ka_ref, ones_ref,
         r_o, k_o, v_o, kk_o, b_o, lw_o, g_o) = refs
    x = x_ref[...]
    xx = xp_ref[...] - x
    mu = mu_ref[...]

    def mix(j):
        return (x + xx * mu[j:j + 1]).astype(BF16)

    r = _dot(mix(0), wr_ref[...])
    z = w0_ref[...] + _dot(jnp.tanh(_dot(mix(1), w1_ref[...])).astype(BF16), w2_ref[...])
    wlog = -(jnp.maximum(-z, 0.0) + jnp.log(1.0 + jnp.exp(-jnp.abs(z)))) - 0.5
    lw = -jnp.exp(wlog)
    kraw = _dot(mix(2), wk_ref[...])
    xv = mix(3)
    v = _dot(xv, wv_ref[...])
    if has_vres:
        gate_v = _sigmoid(v0_ref[...] + _dot(_dot(xv, v1_ref[...]).astype(BF16), v2_ref[...]))
    a = _sigmoid(a0_ref[...] + _dot(_dot(mix(4), a1_ref[...]).astype(BF16), a2_ref[...]))
    g_o[...] = _dot(_sigmoid(_dot(mix(5), g1_ref[...])).astype(BF16), g2_ref[...])
    kkraw = kraw * kk_ref[...]
    kmod = kraw * (1.0 + (a - 1.0) * ka_ref[...])
    ones_blk = ones_ref[...]
    for p in range(r_o.shape[0]):
        sl = slice(p * LANES, (p + 1) * LANES)
        kkp = kkraw[:, sl]
        nrm = jnp.sqrt(_head_sum(kkp * kkp, ones_blk))
        kkn = kkp / jnp.maximum(nrm, 1e-12)
        vp = v[:, sl]
        if has_vres:
            vp = vp + (vf_ref[p] - vp) * gate_v[:, sl]
        r_o[p] = r[:, sl]
        k_o[p] = kmod[:, sl]
        v_o[p] = vp
        kk_o[p] = kkn
        b_o[p] = kkn * a[:, sl]
        lw_o[p] = lw[:, sl]


def _rwkv_proj(x, xprev, prm, vfirst, tm):
    n, d = x.shape
    npair = d // LANES
    has_vres = vfirst is not None
    row = lambda i: (i, 0)
    const2 = lambda i: (0, 0)
    pair_spec = pl.BlockSpec((npair, tm, LANES), lambda i: (0, i, 0))

    def wspec(w):
        return pl.BlockSpec(w.shape, const2)

    args = [x, xprev, prm["mu"], prm["wr"], prm["wk"], prm["wv"], prm["w0"], prm["w1"], prm["w2"],
            prm["a0"], prm["a1"], prm["a2"], prm["g1"], prm["g2"], prm["k_k"], prm["k_a"], prm["ones"]]
    in_specs = [pl.BlockSpec((tm, d), row), pl.BlockSpec((tm, d), row)] + [wspec(w) for w in args[2:]]
    if has_vres:
        args += [prm["v0"], prm["v1"], prm["v2"], vfirst]
        in_specs += [wspec(prm["v0"]), wspec(prm["v1"]), wspec(prm["v2"]), pair_spec]
    pair_shape = jax.ShapeDtypeStruct((npair, n, LANES), F32)
    return pl.pallas_call(
        functools.partial(_rwkv_proj_kernel, has_vres=has_vres),
        grid=(n // tm,),
        in_specs=in_specs,
        out_specs=[pair_spec] * 6 + [pl.BlockSpec((tm, d), row)],
        out_shape=[pair_shape] * 6 + [jax.ShapeDtypeStruct((n, d), F32)],
        compiler_params=_cparams(("parallel",)),
        name="rwkv_proj",
    )(*args)


def _bdot(a, b, dims):
    return lax.dot_general(a.astype(BF16), b.astype(BF16), dims, preferred_element_type=F32)


_NN = (((2,), (1,)), ((0,), (0,)))
_NT = (((2,), (2,)), ((0,), (0,)))
_TN = (((1,), (1,)), ((0,), (0,)))


def _wkv_kernel(r_ref, k_ref, v_ref, kk_ref, b_ref, lw_ref, s0_ref, tri_ref, y_ref, sout_ref, s_scr,
                *, chunks_per_seq, prompt_chunks):
    step = pl.program_id(0)
    in_prompt = step < prompt_chunks
    cidx = step % chunks_per_seq
    first = jnp.logical_or(jnp.logical_not(in_prompt), cidx == 0)
    last = jnp.logical_or(jnp.logical_not(in_prompt), cidx == chunks_per_seq - 1)

    @pl.when(first)
    def _():
        s_scr[...] = s0_ref[0]

    npair = r_ref.shape[0]
    c = r_ref.shape[1]
    lw = lw_ref[...]
    tri = jnp.broadcast_to(tri_ref[...][None], (npair, c, c))
    lw_hi = lw.astype(BF16)
    lw_lo = (lw - lw_hi.astype(F32)).astype(BF16)
    cum = (lax.dot_general(tri, lw_hi, _NN, preferred_element_type=F32)
           + lax.dot_general(tri, lw_lo, _NN, preferred_element_type=F32))
    cum_end = cum[:, c - 1:c, :]
    e_inc = jnp.exp(cum)
    e_exc = jnp.exp(cum - lw)
    e_neg = jnp.exp(-cum)
    e_end = jnp.exp(cum_end - cum)
    p_end = jnp.exp(cum_end)

    lane = lax.broadcasted_iota(jnp.int32, (npair, c, LANES), 2)
    low = lane < HEAD_B

    def stack(z):
        return jnp.concatenate([jnp.where(low, z, 0.0), jnp.where(low, 0.0, z)], axis=1)

    kk = kk_ref[...]
    bb = b_ref[...]
    kmod = k_ref[...]
    a_t = stack(-kk * e_exc)
    r_t = stack(r_ref[...] * e_inc)
    b_t = stack(bb * e_neg)
    k_t = stack(kmod * e_neg)
    b_h = stack(bb * e_end)
    k_h = stack(kmod * e_end)
    v_s = stack(v_ref[...])

    ar = jnp.concatenate([a_t, r_t], axis=1)
    bk = jnp.concatenate([b_t, k_t], axis=1)
    gmat = _bdot(ar, bk, _NT)
    n2 = 2 * c
    ri = lax.broadcasted_iota(jnp.int32, (npair, n2, n2), 1) & (c - 1)
    ci = lax.broadcasted_iota(jnp.int32, (npair, n2, n2), 2) & (c - 1)
    strict = ri > ci
    incl = ri >= ci
    l_m = jnp.where(strict, gmat[:, :n2, :n2], 0.0)
    m_m = jnp.where(strict, gmat[:, :n2, n2:], 0.0)
    rb = jnp.where(incl, gmat[:, n2:, :n2], 0.0)
    rk = jnp.where(incl, gmat[:, n2:, n2:], 0.0)

    s_prev = s_scr[...]
    sa = _bdot(ar, s_prev, _NT)
    w_m = sa[:, :n2] + _bdot(m_m, v_s, _NN)

    eye = (lax.broadcasted_iota(jnp.int32, (npair, n2, n2), 1)
           == lax.broadcasted_iota(jnp.int32, (npair, n2, n2), 2)).astype(F32)
    t_m = eye + l_m
    pw = _bdot(l_m, l_m, _NN)
    span = 2
    while span < c:
        t_m = t_m + _bdot(t_m, pw, _NN)
        span *= 2
        if span < c:
            pw = _bdot(pw, pw, _NN)
    u_s = _bdot(t_m, w_m, _NN)

    y2 = sa[:, n2:] + _bdot(jnp.concatenate([rb, rk], axis=2), jnp.concatenate([u_s, v_s], axis=1), _NN)
    y_ref[...] = y2[:, :c] + y2[:, c:]

    s_new = s_prev * p_end + _bdot(jnp.concatenate([u_s, v_s], axis=1),
                                   jnp.concatenate([b_h, k_h], axis=1), _TN)
    s_scr[...] = s_new

    @pl.when(last)
    def _():
        sout_ref[0] = s_new


def _wkv(r, k, v, kk, b, lw, s0_pairs, n_prompt, seq):
    npair, n, _ = r.shape
    c = CHUNK
    chunks_per_seq = seq // c
    prompt_chunks = n_prompt // c
    nseq = s0_pairs.shape[0]
    n_prompt_seq = n_prompt // seq
    tri = jnp.tril(jnp.ones((c, c), F32)).astype(BF16)

    def seq_map(s):
        return (jnp.where(s < prompt_chunks, s // chunks_per_seq, n_prompt_seq + s - prompt_chunks), 0, 0, 0)

    tok_spec = pl.BlockSpec((npair, c, LANES), lambda s: (0, s, 0))
    state_spec = pl.BlockSpec((1, npair, LANES, LANES), seq_map)
    return pl.pallas_call(
        functools.partial(_wkv_kernel, chunks_per_seq=chunks_per_seq, prompt_chunks=prompt_chunks),
        grid=(n // c,),
        in_specs=[tok_spec] * 6 + [state_spec, pl.BlockSpec((c, c), lambda s: (0, 0))],
        out_specs=[tok_spec, state_spec],
        out_shape=[jax.ShapeDtypeStruct((npair, n, LANES), F32),
                   jax.ShapeDtypeStruct((nseq, npair, LANES, LANES), F32)],
        scratch_shapes=[pltpu.VMEM((npair, LANES, LANES), F32)],
        compiler_params=_cparams(("arbitrary",)),
        name="wkv_chunked",
    )(r, k, v, kk, b, lw, s0_pairs, tri)


def _state_to_pairs(s):
    nseq, h, hv, hk = s.shape
    s = s.reshape(nseq, h // 2, 2, hv, hk)
    z = jnp.zeros_like(s[:, :, 0])
    top = jnp.concatenate([s[:, :, 0], z], axis=-1)
    bot = jnp.concatenate([z, s[:, :, 1]], axis=-1)
    return jnp.concatenate([top, bot], axis=-2)


def _pairs_to_state(sp):
    nseq, npair = sp.shape[:2]
    s0 = sp[:, :, :HEAD_B, :HEAD_B]
    s1 = sp[:, :, HEAD_B:, HEAD_B:]
    return jnp.stack([s0, s1], axis=2).reshape(nseq, 2 * npair, HEAD_B, HEAD_B)


def _rwkv_out_kernel(y_ref, r_ref, k_ref, v_ref, g_ref, x_ref, lg_ref, lb_ref, rk_ref, ones_ref,
                     wo_ref, ng_ref, nb_ref, o_ref):
    ones_blk = ones_ref[...]
    inv = 1.0 / HEAD_B
    parts = []
    for p in range(y_ref.shape[0]):
        sl = slice(p * LANES, (p + 1) * LANES)
        y = y_ref[p]
        mean = _head_sum(y, ones_blk) * inv
        yc = y - mean
        var = _head_sum(yc * yc, ones_blk) * inv
        yn = yc * lax.rsqrt(var + GN_EPS) * lg_ref[:, sl] + lb_ref[:, sl]
        bonus = _head_sum(r_ref[p] * k_ref[p] * rk_ref[:, sl], ones_blk) * v_ref[p]
        parts.append(((yn + bonus) * g_ref[:, sl]).astype(BF16))
    z = jnp.concatenate(parts, axis=1)
    h = _dot(z, wo_ref[...])
    o_ref[...] = _layer_norm(ALPHA * x_ref[...] + h, ng_ref[...], nb_ref[...])


def _rwkv_out(y, r, k, v, g, x, prm, ln_g, ln_b, tm):
    n, d = x.shape
    npair = d // LANES
    row = lambda i: (i, 0)
    const2 = lambda i: (0, 0)
    pair_spec = pl.BlockSpec((npair, tm, LANES), lambda i: (0, i, 0))
    vec = pl.BlockSpec((1, d), const2)
    return pl.pallas_call(
        _rwkv_out_kernel,
        grid=(n // tm,),
        in_specs=[pair_spec] * 4 + [pl.BlockSpec((tm, d), row), pl.BlockSpec((tm, d), row),
                                    vec, vec, vec, pl.BlockSpec((LANES, LANES), const2),
                                    pl.BlockSpec((d, d), const2), vec, vec],
        out_specs=pl.BlockSpec((tm, d), row),
        out_shape=jax.ShapeDtypeStruct((n, d), F32),
        compiler_params=_cparams(("parallel",)),
        name="rwkv_out",
    )(y, r, k, v, g, x, prm["lnx_g"], prm["lnx_b"], prm["r_k"], prm["ones"], prm["wo"], ln_g, ln_b)


def _pad_cols(w, mult):
    pad = (-w.shape[1]) % mult
    return jnp.pad(w, ((0, 0), (0, pad))) if pad else w


def _pad_rows(w, mult):
    pad = (-w.shape[0]) % mult
    return jnp.pad(w, ((0, pad), (0, 0))) if pad else w


def _rwkv_params(j, mu_b, w_rkv_b, w0_b, w1_b, w2_b, a0_b, a1_b, a2_b, v0_b, v1_b, v2_b, g1_b, g2_b,
                 k_k_b, k_a_b, r_k_b, lnx_g_b, lnx_b_b, w_o_b):
    d = mu_b.shape[-1]
    lane = jnp.arange(LANES)
    ones = (lane[:, None] // HEAD_B == lane[None, :] // HEAD_B).astype(BF16)
    prm = dict(
        mu=mu_b[j], wr=w_rkv_b[j, 0].astype(BF16), wk=w_rkv_b[j, 1].astype(BF16), wv=w_rkv_b[j, 2].astype(BF16),
        w0=w0_b[j].reshape(1, d), w1=_pad_cols(w1_b[j], LANES).astype(BF16), w2=_pad_rows(w2_b[j], LANES).astype(BF16),
        a0=a0_b[j].reshape(1, d), a1=_pad_cols(a1_b[j], LANES).astype(BF16), a2=_pad_rows(a2_b[j], LANES).astype(BF16),
        g1=_pad_cols(g1_b[j], LANES).astype(BF16), g2=_pad_rows(g2_b[j], LANES).astype(BF16),
        k_k=k_k_b[j].reshape(1, d), k_a=k_a_b[j].reshape(1, d), r_k=r_k_b[j].reshape(1, d),
        lnx_g=lnx_g_b[j].reshape(1, d), lnx_b=lnx_b_b[j].reshape(1, d), wo=w_o_b[j].astype(BF16), ones=ones,
    )
    if j > 0:
        prm.update(v0=v0_b[j - 1].reshape(1, d), v1=_pad_cols(v1_b[j - 1], LANES).astype(BF16),
                   v2=_pad_rows(v2_b[j - 1], LANES).astype(BF16))
    return prm


def kernel(x_prompt, x_sample, cache_k_a, cache_v_a, state_shift_b, state_wkv_b, w_qkv_a, sinks_a, w_o_a, mu_b, w_rkv_b, w0_b, w1_b, w2_b, a0_b, a1_b, a2_b, v0_b, v1_b, v2_b, g1_b, g2_b, k_k_b, k_a_b, r_k_b, lnx_g_b, lnx_b_b, w_o_b, ln_g, ln_b, w_rg, b_rg, w_re, b_re, w_exp_in, w_exp_out):
    bp, seq, d = x_prompt.shape
    bs, dec_seq, _ = x_sample.shape
    n_p, n_s = bp * seq, bs * dec_seq
    n = n_p + n_s
    nk = N_KV_HEADS * HEAD_DIM
    depth = ln_g.shape[0]

    x = jnp.concatenate([x_prompt.reshape(n_p, d), x_sample.reshape(n_s, d)], axis=0)

    tm_qkv = _pick_tile((512, 256, 128, 64), divides=(seq, n_s), multiple_of=(dec_seq,))
    tm_ln = _pick_tile((512, 256, 128, 64), divides=(n,))
    tm_rw = _pick_tile((256, 128, 64), divides=(n,))
    qb = _pick_tile((256, 128), divides=(seq,))
    cos_t, sin_t = _rope_tables(seq, dec_seq, tm_qkv)

    kp_l, vp_l, ks_l, vs_l = [], [], [], []
    shp_l, stp_l, shs_l, sts_l = [], [], [], []
    vfirst = None
    for i in range(depth):
        j = i // 2
        if i % 2 == 0:
            q, k, v = _qkv_rope(x, w_qkv_a[j].astype(BF16), cos_t, sin_t, tm_qkv, n_p, seq)
            qp = q[:n_p].reshape(bp, seq, -1)
            kp = k[:n_p].reshape(bp, seq, nk)
            vp = v[:n_p].reshape(bp, seq, nk)
            qs = q[n_p:].reshape(bs, dec_seq, -1)
            ks = k[n_p:].reshape(bs, dec_seq, nk)
            vs = v[n_p:].reshape(bs, dec_seq, nk)
            op = _attention(qp, kp, kp, vp, vp, sinks_a[j], qb, True, True)
            os_ = _attention(qs, cache_k_a[j].reshape(bs, WINDOW_ROWS, nk), ks,
                             cache_v_a[j].reshape(bs, WINDOW_ROWS, nk), vs, sinks_a[j], dec_seq, False, False)
            o = jnp.concatenate([op.reshape(n_p, -1), os_.reshape(n_s, -1)], axis=0)
            x = _proj_ln(o, w_o_a[j].astype(BF16), x, ln_g[i, 0].reshape(1, d), ln_b[i, 0].reshape(1, d), tm_ln)
            keep = min(WINDOW_ROWS, seq)
            kp_l.append(kp[:, seq - keep:].reshape(bp, keep, N_KV_HEADS, HEAD_DIM))
            vp_l.append(vp[:, seq - keep:].reshape(bp, keep, N_KV_HEADS, HEAD_DIM))
            ks_l.append(ks.reshape(bs, dec_seq, N_KV_HEADS, HEAD_DIM))
            vs_l.append(vs.reshape(bs, dec_seq, N_KV_HEADS, HEAD_DIM))
        else:
            prm = _rwkv_params(j, mu_b, w_rkv_b, w0_b, w1_b, w2_b, a0_b, a1_b, a2_b, v0_b, v1_b, v2_b,
                               g1_b, g2_b, k_k_b, k_a_b, r_k_b, lnx_g_b, lnx_b_b, w_o_b)
            xp3 = x[:n_p].reshape(bp, seq, d)
            xs3 = x[n_p:].reshape(bs, dec_seq, d)
            xprev = jnp.concatenate([
                jnp.concatenate([jnp.zeros((bp, 1, d), F32), xp3[:, :-1]], axis=1).reshape(n_p, d),
                jnp.concatenate([state_shift_b[j][:, None], xs3[:, :-1]], axis=1).reshape(n_s, d)], axis=0)
            r, k, v, kk, b, lw, g = _rwkv_proj(x, xprev, prm, vfirst, tm_rw)
            if vfirst is None:
                vfirst = v
            s0 = jnp.concatenate([jnp.zeros((bp,) + state_wkv_b.shape[2:], F32), state_wkv_b[j]], axis=0)
            y, s_fin = _wkv(r, k, v, kk, b, lw, _state_to_pairs(s0), n_p, seq)
            s_fin = _pairs_to_state(s_fin)
            shp_l.append(xp3[:, -1])
            shs_l.append(xs3[:, -1])
            stp_l.append(s_fin[:bp])
            sts_l.append(s_fin[bp:])
            x = _rwkv_out(y, r, k, v, g, x, prm, ln_g[i, 0].reshape(1, d), ln_b[i, 0].reshape(1, d), tm_rw)
        w_router = jnp.pad(jnp.concatenate([w_re[i], w_rg[i]], axis=1),
                           ((0, 0), (0, LANES - N_EXPERTS - N_GROUPS))).astype(BF16)
        b_router = jnp.pad(jnp.concatenate([b_re[i], b_rg[i]]), (0, LANES - N_EXPERTS - N_GROUPS)).reshape(1, LANES)
        x = _hier_moe_ln(x, w_router, b_router, w_exp_in[i].astype(BF16), w_exp_out[i].astype(BF16),
                         ln_g[i, 1].reshape(1, d), ln_b[i, 1].reshape(1, d))

    return (x[:n_p].reshape(bp, seq, d), x[n_p:].reshape(bs, dec_seq, d),
            jnp.stack(kp_l), jnp.stack(vp_l), jnp.stack(shp_l), jnp.stack(stp_l),
            jnp.stack(ks_l), jnp.stack(vs_l), jnp.stack(shs_l), jnp.stack(sts_l))
```

```python
import functools

import jax
import jax.numpy as jnp
from jax import lax
from jax.experimental import pallas as pl
from jax.experimental.pallas import tpu as pltpu

F32 = jnp.float32
BF16 = jnp.bfloat16

CHUNK = 64
HEAD_DIM = 64
N_Q_HEADS = 16
N_KV_HEADS = 4
WINDOW_ROWS = 128
ROPE_THETA = 10000.0
PAST_LEN = 2048
HEAD_B = 64
GN_EPS = 64e-5
N_GROUPS = 4
E_PER_GROUP = 8
N_EXPERTS = 32
D_EXPERT = 512
LN_EPS = 1e-5
DEPTH = 4
ALPHA = (2 * DEPTH) ** 0.25

LANES = 128
VMEM_LIMIT = 56 * 1024 * 1024
NEG_BIG = -1e30


def _cparams(sem):
    return pltpu.CompilerParams(dimension_semantics=sem, vmem_limit_bytes=VMEM_LIMIT)


def _pick_tile(cands, divides=(), multiple_of=()):
    for t in cands:
        if all(n % t == 0 for n in divides) and all(t % m == 0 for m in multiple_of):
            return t
    raise ValueError(f"no tile in {cands} for {divides} / {multiple_of}")


def _layer_norm(z, g, b):
    mu = jnp.mean(z, axis=-1, keepdims=True)
    zc = z - mu
    var = jnp.mean(zc * zc, axis=-1, keepdims=True)
    return zc * lax.rsqrt(var + LN_EPS) * g + b


def _dot(a, b):
    return jnp.dot(a, b, preferred_element_type=F32)


def _sigmoid(z):
    return 1.0 / (1.0 + jnp.exp(-z))


def _qkv_kernel(x_ref, w_ref, cos_ref, sin_ref, q_ref, k_ref, v_ref):
    nq = q_ref.shape[1]
    nk = k_ref.shape[1]
    acc = _dot(x_ref[...].astype(BF16), w_ref[...])
    cos = cos_ref[...]
    sin = sin_ref[...]
    lane = lax.broadcasted_iota(jnp.int32, cos.shape, 1)
    first_half = (lane & (HEAD_DIM - 1)) < HEAD_DIM // 2

    def rope(xg):
        rot = jnp.where(first_half, pltpu.roll(xg, LANES - HEAD_DIM // 2, 1),
                        pltpu.roll(xg, HEAD_DIM // 2, 1))
        return xg * cos + rot * sin

    for g in range(nq // LANES):
        sl = slice(g * LANES, (g + 1) * LANES)
        q_ref[:, sl] = rope(acc[:, sl]).astype(q_ref.dtype)
    for g in range(nk // LANES):
        k_ref[:, g * LANES:(g + 1) * LANES] = rope(acc[:, nq + g * LANES:nq + (g + 1) * LANES])
    v_ref[...] = acc[:, nq + nk:]


def _qkv_rope(x, w_bf, cos_t, sin_t, tm, n_prompt, seq):
    n, d = x.shape
    nq, nk = N_Q_HEADS * HEAD_DIM, N_KV_HEADS * HEAD_DIM
    tiles_per_seq = seq // tm
    prompt_tiles = n_prompt // tm

    def tab_map(i):
        return (jnp.where(i < prompt_tiles, i % tiles_per_seq, tiles_per_seq), 0)

    return pl.pallas_call(
        _qkv_kernel,
        grid=(n // tm,),
        in_specs=[
            pl.BlockSpec((tm, d), lambda i: (i, 0)),
            pl.BlockSpec((d, nq + 2 * nk), lambda i: (0, 0)),
            pl.BlockSpec((tm, LANES), tab_map),
            pl.BlockSpec((tm, LANES), tab_map),
        ],
        out_specs=[
            pl.BlockSpec((tm, nq), lambda i: (i, 0)),
            pl.BlockSpec((tm, nk), lambda i: (i, 0)),
            pl.BlockSpec((tm, nk), lambda i: (i, 0)),
        ],
        out_shape=[
            jax.ShapeDtypeStruct((n, nq), BF16),
            jax.ShapeDtypeStruct((n, nk), F32),
            jax.ShapeDtypeStruct((n, nk), F32),
        ],
        compiler_params=_cparams(("parallel",)),
        name="qkv_rope",
    )(x, w_bf, cos_t, sin_t)


def _rope_tables(seq, dec_seq, tm):
    inv = ROPE_THETA ** (-(jnp.arange(0, HEAD_DIM, 2, dtype=F32) / HEAD_DIM))
    pos = jnp.concatenate([jnp.arange(seq), PAST_LEN + (jnp.arange(tm) % dec_seq)]).astype(F32)
    ang = pos[:, None] * inv[None, :]
    cos, sin = jnp.cos(ang), jnp.sin(ang)
    reps = LANES // HEAD_DIM
    cos_t = jnp.tile(jnp.concatenate([cos, cos], axis=1), (1, reps))
    sin_t = jnp.tile(jnp.concatenate([-sin, sin], axis=1), (1, reps))
    return cos_t, sin_t


def _attn_kernel(sinks_ref, q_ref, kh_ref, km_ref, vh_ref, vm_ref, o_ref, *, qb, mask_halo):
    blk = pl.program_id(1)
    scale = HEAD_DIM ** -0.5
    gq = N_Q_HEADS // N_KV_HEADS
    kall = jnp.concatenate([kh_ref[...], km_ref[...]], axis=0)
    vall = jnp.concatenate([vh_ref[...], vm_ref[...]], axis=0)
    rows = kall.shape[0]
    low_kv = lax.broadcasted_iota(jnp.int32, (rows, LANES), 1) < HEAD_DIM
    low_q = lax.broadcasted_iota(jnp.int32, (CHUNK, LANES), 1) < HEAD_DIM
    band = WINDOW_ROWS + CHUNK
    key_row = lax.broadcasted_iota(jnp.int32, (gq * CHUNK, band), 1)
    head_row = lax.broadcasted_iota(jnp.int32, (gq * CHUNK, 1), 0) >> (CHUNK.bit_length() - 1)

    both = []
    for kvh in range(N_KV_HEADS):
        sl = slice(LANES * (kvh // 2), LANES * (kvh // 2 + 1))
        own = low_kv if kvh % 2 == 0 else jnp.logical_not(low_kv)
        per = []
        for arr in (kall, vall):
            a_own = jnp.where(own, arr[:, sl], 0.0)
            per.append((a_own + pltpu.roll(a_own, HEAD_DIM, 1)).astype(BF16))
        both.append(per)

    for kvh in range(N_KV_HEADS):
        k2, v2 = both[kvh]
        sink = jnp.zeros((gq * CHUNK, 1), F32)
        for h in range(gq):
            sink = jnp.where(head_row == h, sinks_ref[gq * kvh + h], sink)
        for c in range(qb // CHUNK):
            r0 = c * CHUNK
            parts = []
            for pair in range(gq // 2):
                col = LANES * (2 * kvh + pair)
                q2 = q_ref[r0:r0 + CHUNK, col:col + LANES]
                parts.append(jnp.where(low_q, q2, jnp.zeros_like(q2)))
                parts.append(jnp.where(low_q, jnp.zeros_like(q2), q2))
            q4 = jnp.concatenate(parts, axis=0)
            s = lax.dot_general(q4, k2[r0:r0 + band], (((1,), (1,)), ((), ())),
                                preferred_element_type=F32) * scale
            if mask_halo and r0 < WINDOW_ROWS:
                s = jnp.where(key_row >= jnp.where(blk > 0, 0, WINDOW_ROWS - r0), s, NEG_BIG)
            m = jnp.maximum(jnp.max(s, axis=-1, keepdims=True), sink)
            p = jnp.exp(s - m)
            p = p * (1.0 / (jnp.sum(p, axis=-1, keepdims=True) + jnp.exp(sink - m)))
            o4 = _dot(p.astype(BF16), v2[r0:r0 + band])
            for pair in range(gq // 2):
                col = LANES * (2 * kvh + pair)
                lo = o4[(2 * pair) * CHUNK:(2 * pair + 1) * CHUNK]
                hi = o4[(2 * pair + 1) * CHUNK:(2 * pair + 2) * CHUNK]
                o_ref[r0:r0 + CHUNK, col:col + LANES] = jnp.where(low_q, lo, hi).astype(o_ref.dtype)


def _attention(q, k_halo, k_main, v_halo, v_main, sinks, n_seq, seq, qb, row0, mask_halo, halo_from_main):
    nq = q.shape[1]
    nk = k_main.shape[1]
    blocks = seq // qb
    base = row0 // qb
    hb = qb // WINDOW_ROWS

    def main_map(bi, j):
        return (base + bi * blocks + j, 0)

    def halo_map(bi, j):
        if halo_from_main:
            return (jnp.maximum((base + bi * blocks + j) * hb - 1, 0), 0)
        return (bi, 0)

    return pl.pallas_call(
        functools.partial(_attn_kernel, qb=qb, mask_halo=mask_halo),
        grid=(n_seq, blocks),
        in_specs=[
            pl.BlockSpec(memory_space=pltpu.SMEM),
            pl.BlockSpec((qb, nq), main_map),
            pl.BlockSpec((WINDOW_ROWS, nk), halo_map),
            pl.BlockSpec((qb, nk), main_map),
            pl.BlockSpec((WINDOW_ROWS, nk), halo_map),
            pl.BlockSpec((qb, nk), main_map),
        ],
        out_specs=pl.BlockSpec((qb, nq), lambda bi, j: (bi * blocks + j, 0)),
        out_shape=jax.ShapeDtypeStruct((n_seq * seq, nq), BF16),
        compiler_params=_cparams(("parallel", "parallel")),
        name="attn_prompt" if mask_halo else "attn_sample",
    )(sinks, q, k_halo, k_main, v_halo, v_main)


def _store_token_tiles(t_ref, z):
    for j in range(t_ref.shape[1]):
        t_ref[:, j, :] = z[:, j * LANES:(j + 1) * LANES]


def _load_token_tiles(t_ref, lead, row0, rows):
    return jnp.concatenate([t_ref[lead, row0:row0 + rows, j, :] for j in range(t_ref.shape[2])], axis=1)


def _proj_ln_kernel(ap_ref, as_ref, w_ref, x_ref, g_ref, b_ref, o_ref, t_ref, *, prompt_tiles):
    a = jnp.where(pl.program_id(0) < prompt_tiles, ap_ref[...], as_ref[...])
    h = _dot(a, w_ref[...])
    z = _layer_norm(ALPHA * x_ref[...] + h, g_ref[...], b_ref[...])
    o_ref[...] = z
    _store_token_tiles(t_ref, z)


def _proj_ln(a_prompt, a_sample, w_bf, x, g, b, tm):
    n, d = x.shape
    ka = a_prompt.shape[1]
    pt = a_prompt.shape[0] // tm
    st = a_sample.shape[0] // tm
    return pl.pallas_call(
        functools.partial(_proj_ln_kernel, prompt_tiles=pt),
        grid=(n // tm,),
        in_specs=[
            pl.BlockSpec((tm, ka), lambda i: (jnp.minimum(i, pt - 1), 0)),
            pl.BlockSpec((tm, ka), lambda i: (jnp.clip(i - pt, 0, st - 1), 0)),
            pl.BlockSpec((ka, d), lambda i: (0, 0)),
            pl.BlockSpec((tm, d), lambda i: (i, 0)),
            pl.BlockSpec((1, d), lambda i: (0, 0)),
            pl.BlockSpec((1, d), lambda i: (0, 0)),
        ],
        out_specs=[pl.BlockSpec((tm, d), lambda i: (i, 0)),
                   pl.BlockSpec((tm, d // LANES, LANES), lambda i: (i, 0, 0))],
        out_shape=[jax.ShapeDtypeStruct((n, d), F32),
                   jax.ShapeDtypeStruct((n, d // LANES, LANES), F32)],
        compiler_params=_cparams(("parallel",)),
        name="proj_ln",
    )(a_prompt, a_sample, w_bf, x, g, b)


GROUP_LANE0 = N_EXPERTS


def _router_kernel(x_ref, w_ref, b_ref, o_ref):
    lg = _dot(x_ref[...].astype(BF16), w_ref[...]) + b_ref[...]
    lane = lax.broadcasted_iota(jnp.int32, lg.shape, 1)
    lane_f = lane.astype(F32)
    is_grp = jnp.logical_and(lane >= GROUP_LANE0, lane < GROUP_LANE0 + N_GROUPS)

    def top1(mask):
        m = jnp.max(jnp.where(mask, lg, NEG_BIG), axis=-1, keepdims=True)
        idx = jnp.min(jnp.where(jnp.logical_and(mask, lg == m), lane_f, 1e9), axis=-1, keepdims=True)
        return m, idx

    gm, gidx = top1(is_grp)
    grp = gidx - GROUP_LANE0
    pg = 1.0 / jnp.sum(jnp.where(is_grp, jnp.exp(lg - gm), 0.0), axis=-1, keepdims=True)
    in_grp = jnp.logical_and(lane < N_EXPERTS, (lane >> 3).astype(F32) == grp)
    m1, i1 = top1(in_grp)
    m2, i2 = top1(jnp.logical_and(in_grp, lane_f != i1))
    t = jnp.exp(m2 - m1)
    s1 = 1.0 / (1.0 + t)
    s2 = t / (1.0 + t)
    out = jnp.where(lane == 0, i1, jnp.where(lane == 1, i2,
          jnp.where(lane == 2, pg * s1, jnp.where(lane == 3, pg * s2, 0.0))))
    o_ref[...] = out


def _router(x, w_bf, bias, tm):
    n, d = x.shape
    return pl.pallas_call(
        _router_kernel,
        grid=(n // tm,),
        in_specs=[
            pl.BlockSpec((tm, d), lambda i: (i, 0)),
            pl.BlockSpec((d, LANES), lambda i: (0, 0)),
            pl.BlockSpec((1, LANES), lambda i: (0, 0)),
        ],
        out_specs=pl.BlockSpec((tm, LANES), lambda i: (i, 0)),
        out_shape=jax.ShapeDtypeStruct((n, LANES), F32),
        compiler_params=_cparams(("parallel",)),
        name="moe_router",
    )(x, w_bf, bias)


def _row_copy(src_hbm, dst, sem, src_row, dst_row):
    return pltpu.make_async_copy(src_hbm.at[pl.ds(src_row, 1)], dst.at[pl.ds(dst_row, 1)], sem)


def _gather_rows(idx_ref, src_hbm, dst, sem, n_rows):
    def body(r, carry):
        _row_copy(src_hbm, dst, sem, idx_ref[0, 0, r], r).start()
        return carry
    lax.fori_loop(0, n_rows, body, 0, unroll=8)


def _wait_rows(src_hbm, dst, sem, n_rows):
    def body(r, carry):
        _row_copy(src_hbm, dst, sem, 0, r).wait()
        return carry
    lax.fori_loop(0, n_rows, body, 0, unroll=8)


def _expert_kernel(blk_e_ref, nvalid_ref, tok_ref, tok_next_ref, x_hbm, win_ref, wout_ref,
                   ys_ref, buf, sem):
    i = pl.program_id(0)
    nb = pl.num_programs(0)
    bm = buf.shape[1]
    slot = i % 2
    nvalid = nvalid_ref[0]

    @pl.when(jnp.logical_and(i == 0, nvalid > 0))
    def _():
        _gather_rows(tok_ref, x_hbm, buf.at[0], sem.at[0], bm)

    @pl.when(jnp.logical_and(i + 1 < nb, i + 1 < nvalid))
    def _():
        nslot = (i + 1) % 2
        _gather_rows(tok_next_ref, x_hbm, buf.at[nslot], sem.at[nslot], bm)

    @pl.when(i < nvalid)
    def _():
        _wait_rows(x_hbm, buf.at[slot], sem.at[slot], bm)
        xb = _load_token_tiles(buf, slot, 0, bm).astype(BF16)
        h = _dot(xb, win_ref[0])
        h1 = h[:, :D_EXPERT]
        act = h1 * _sigmoid(h1) * h[:, D_EXPERT:]
        _store_token_tiles(ys_ref, _dot(act.astype(BF16), wout_ref[0]))

    @pl.when(i >= nvalid)
    def _():
        ys_ref[...] = jnp.zeros_like(ys_ref)


def _experts(xt, row_tok, blk_e, nvalid, w_in_bf, w_out_bf, bm):
    n, sub, _ = xt.shape
    d = sub * LANES
    nb = row_tok.shape[0]
    grid_spec = pltpu.PrefetchScalarGridSpec(
        num_scalar_prefetch=2,
        grid=(nb,),
        in_specs=[
            pl.BlockSpec((1, 1, bm), lambda i, be, nv: (i, 0, 0), memory_space=pltpu.SMEM),
            pl.BlockSpec((1, 1, bm), lambda i, be, nv: (jnp.minimum(i + 1, nb - 1), 0, 0),
                         memory_space=pltpu.SMEM),
            pl.BlockSpec(memory_space=pl.ANY),
            pl.BlockSpec((1, d, 2 * D_EXPERT), lambda i, be, nv: (be[i], 0, 0)),
            pl.BlockSpec((1, D_EXPERT, d), lambda i, be, nv: (be[i], 0, 0)),
        ],
        out_specs=pl.BlockSpec((bm, sub, LANES), lambda i, be, nv: (i, 0, 0)),
        scratch_shapes=[pltpu.VMEM((2, bm, sub, LANES), F32), pltpu.SemaphoreType.DMA((2,))],
    )
    return pl.pallas_call(
        _expert_kernel,
        grid_spec=grid_spec,
        out_shape=jax.ShapeDtypeStruct((nb * bm, sub, LANES), F32),
        compiler_params=_cparams(("arbitrary",)),
        name="moe_experts",
    )(blk_e, nvalid, row_tok, row_tok, xt, w_in_bf, w_out_bf)


def _combine_kernel(pos_ref, pos_next_ref, ys_hbm, route_ref, x_ref, g_ref, b_ref, o_ref, buf, sem):
    i = pl.program_id(0)
    nb = pl.num_programs(0)
    tm = x_ref.shape[0]
    slot = i % 2

    @pl.when(i == 0)
    def _():
        _gather_rows(pos_ref, ys_hbm, buf.at[0], sem.at[0], 2 * tm)

    @pl.when(i + 1 < nb)
    def _():
        nslot = (i + 1) % 2
        _gather_rows(pos_next_ref, ys_hbm, buf.at[nslot], sem.at[nslot], 2 * tm)

    _wait_rows(ys_hbm, buf.at[slot], sem.at[slot], 2 * tm)
    route = route_ref[...]
    y = (_load_token_tiles(buf, slot, 0, tm) * route[:, 2:3]
         + _load_token_tiles(buf, slot, tm, tm) * route[:, 3:4])
    o_ref[...] = _layer_norm(ALPHA * x_ref[...] + y, g_ref[...], b_ref[...])


def _combine_ln(ys, pos_tab, route, x, g, b, tm):
    n, d = x.shape
    nb = n // tm
    sub = ys.shape[1]
    return pl.pallas_call(
        _combine_kernel,
        grid=(nb,),
        in_specs=[
            pl.BlockSpec((1, 1, 2 * tm), lambda i: (i, 0, 0), memory_space=pltpu.SMEM),
            pl.BlockSpec((1, 1, 2 * tm), lambda i: (jnp.minimum(i + 1, nb - 1), 0, 0),
                         memory_space=pltpu.SMEM),
            pl.BlockSpec(memory_space=pl.ANY),
            pl.BlockSpec((tm, LANES), lambda i: (i, 0)),
            pl.BlockSpec((tm, d), lambda i: (i, 0)),
            pl.BlockSpec((1, d), lambda i: (0, 0)),
            pl.BlockSpec((1, d), lambda i: (0, 0)),
        ],
        out_specs=pl.BlockSpec((tm, d), lambda i: (i, 0)),
        out_shape=jax.ShapeDtypeStruct((n, d), F32),
        scratch_shapes=[pltpu.VMEM((2, 2 * tm, sub, LANES), F32), pltpu.SemaphoreType.DMA((2,))],
        compiler_params=_cparams(("arbitrary",)),
        name="moe_combine_ln",
    )(pos_tab, pos_tab, ys, route, x, g, b)


def _moe_plan(route, bm, tm):
    n = route.shape[0]
    eid = route[:, :2].astype(jnp.int32).reshape(-1)
    na = eid.shape[0]
    onehot = (eid[:, None] == jnp.arange(N_EXPERTS, dtype=jnp.int32)[None, :]).astype(jnp.int32)
    csum = jnp.cumsum(onehot, axis=0)
    rank = jnp.sum(csum * onehot, axis=1) - 1
    counts = csum[-1]
    padded = (counts + bm - 1) // bm * bm
    pad_end = jnp.cumsum(padded)
    pad_off = pad_end - padded
    start = jnp.cumsum(counts) - counts
    dest = pad_off[eid] + rank
    nb = -(-na // bm) + N_EXPERTS
    order = jnp.argsort(eid, stable=True).astype(jnp.int32)
    blk_row0 = jnp.arange(nb, dtype=jnp.int32) * bm
    blk_e = jnp.minimum(jnp.sum((blk_row0[:, None] >= pad_end[None, :]).astype(jnp.int32), axis=1),
                        N_EXPERTS - 1)
    blk_local0 = blk_row0 - pad_off[blk_e]
    blk_left = jnp.where(blk_row0 < pad_end[-1], counts[blk_e] - blk_local0, 0)
    j = jnp.arange(bm, dtype=jnp.int32)
    src = jnp.clip((start[blk_e] + blk_local0)[:, None] + j[None, :], 0, na - 1)
    row_tok = jnp.where(j[None, :] < blk_left[:, None], order[src] // 2, 0).astype(jnp.int32).reshape(nb, 1, bm)
    nvalid = (pad_end[-1] // bm).astype(jnp.int32).reshape(1)
    pos_tab = dest.astype(jnp.int32).reshape(n // tm, tm, 2).transpose(0, 2, 1).reshape(n // tm, 1, 2 * tm)
    return row_tok, blk_e, nvalid, pos_tab


def _hier_moe_ln(x, xt, w_router_bf, b_router, w_in_bf, w_out_bf, g, b):
    n = x.shape[0]
    tm_r = _pick_tile((512, 256, 128, 64), divides=(n,))
    tm_c = _pick_tile((256, 128, 64), divides=(n,))
    bm = 256 if (2 * n) % 256 == 0 else 128
    route = _router(x, w_router_bf, b_router, tm_r)
    row_tok, blk_e, nvalid, pos_tab = _moe_plan(route, bm, tm_c)
    ys = _experts(xt, row_tok, blk_e, nvalid, w_in_bf, w_out_bf, bm)
    return _combine_ln(ys, pos_tab, route, x, g, b, tm_c)


def _head_sum(z, ones_blk):
    hi = z.astype(BF16)
    lo = (z - hi.astype(F32)).astype(BF16)
    return _dot(hi, ones_blk) + _dot(lo, ones_blk)


def _rwkv_proj_kernel(*refs, has_vres):
    if has_vres:
        (x_ref, xp_ref, mu_ref, wr_ref, wk_ref, wv_ref, w0_ref, w1_ref, w2_ref, a0_ref, a1_ref, a2_ref,
         g1_ref, g2_ref, kk_ref, ka_ref, ones_ref, v0_ref, v1_ref, v2_ref, vf_ref,
         r_o, k_o, v_o, kk_o, b_o, lw_o, g_o) = refs
    else:
        (x_ref, xp_ref, mu_ref, wr_ref, wk_ref, wv_ref, w0_ref, w1_ref, w2_ref, a0_ref, a1_ref, a2_ref,
         g1_ref, g2_ref, kk_ref, ka_ref, ones_ref,
         r_o, k_o, v_o, kk_o, b_o, lw_o, g_o) = refs
    x = x_ref[...]
    tm = x.shape[0]
    heads_of_chunks = jnp.concatenate(
        [jnp.broadcast_to(xp_ref[0, c:c + 1, :], (CHUNK, x.shape[1])) for c in range(tm // CHUNK)], axis=0)
    chunk_row = lax.broadcasted_iota(jnp.int32, (tm, 1), 0) & (CHUNK - 1)
    xx = jnp.where(chunk_row == 0, heads_of_chunks, pltpu.roll(x, 1, 0)) - x
    mu = mu_ref[...]

    def mix(j):
        return (x + xx * mu[j:j + 1]).astype(BF16)

    r = _dot(mix(0), wr_ref[...])
    z = w0_ref[...] + _dot(jnp.tanh(_dot(mix(1), w1_ref[...])).astype(BF16), w2_ref[...])
    wlog = -(jnp.maximum(-z, 0.0) + jnp.log(1.0 + jnp.exp(-jnp.abs(z)))) - 0.5
    lw = -jnp.exp(wlog)
    kraw = _dot(mix(2), wk_ref[...])
    xv = mix(3)
    v = _dot(xv, wv_ref[...])
    if has_vres:
        gate_v = _sigmoid(v0_ref[...] + _dot(_dot(xv, v1_ref[...]).astype(BF16), v2_ref[...]))
    a = _sigmoid(a0_ref[...] + _dot(_dot(mix(4), a1_ref[...]).astype(BF16), a2_ref[...]))
    g_o[...] = _dot(_sigmoid(_dot(mix(5), g1_ref[...])).astype(BF16), g2_ref[...])
    kkraw = kraw * kk_ref[...]
    kmod = kraw * (1.0 + (a - 1.0) * ka_ref[...])
    ones_blk = ones_ref[...]
    for p in range(r_o.shape[0]):
        sl = slice(p * LANES, (p + 1) * LANES)
        kkp = kkraw[:, sl]
        nrm = jnp.sqrt(_head_sum(kkp * kkp, ones_blk))
        kkn = kkp / jnp.maximum(nrm, 1e-12)
        vp = v[:, sl]
        if has_vres:
            vp = vp + (vf_ref[p] - vp) * gate_v[:, sl]
        r_o[p] = r[:, sl]
        k_o[p] = kmod[:, sl]
        v_o[p] = vp
        kk_o[p] = kkn
        b_o[p] = kkn * a[:, sl]
        lw_o[p] = lw[:, sl]


def _rwkv_proj(x, chunk_prev, prm, vfirst, tm):
    n, d = x.shape
    npair = d // LANES
    has_vres = vfirst is not None
    row = lambda i: (i, 0)
    const2 = lambda i: (0, 0)
    pair_spec = pl.BlockSpec((npair, tm, LANES), lambda i: (0, i, 0))

    def wspec(w):
        return pl.BlockSpec(w.shape, const2)

    args = [x, chunk_prev, prm["mu"], prm["wr"], prm["wk"], prm["wv"], prm["w0"], prm["w1"], prm["w2"],
            prm["a0"], prm["a1"], prm["a2"], prm["g1"], prm["g2"], prm["k_k"], prm["k_a"], prm["ones"]]
    in_specs = ([pl.BlockSpec((tm, d), row), pl.BlockSpec((1, tm // CHUNK, d), lambda i: (i, 0, 0))]
                + [wspec(w) for w in args[2:]])
    if has_vres:
        args += [prm["v0"], prm["v1"], prm["v2"], vfirst]
        in_specs += [wspec(prm["v0"]), wspec(prm["v1"]), wspec(prm["v2"]), pair_spec]
    pair_shape = jax.ShapeDtypeStruct((npair, n, LANES), F32)
    return pl.pallas_call(
        functools.partial(_rwkv_proj_kernel, has_vres=has_vres),
        grid=(n // tm,),
        in_specs=in_specs,
        out_specs=[pair_spec] * 6 + [pl.BlockSpec((tm, d), row)],
        out_shape=[pair_shape] * 6 + [jax.ShapeDtypeStruct((n, d), F32)],
        compiler_params=_cparams(("parallel",)),
        name="rwkv_proj",
    )(*args)


def _bdot(a, b, dims):
    return lax.dot_general(a.astype(BF16), b.astype(BF16), dims, preferred_element_type=F32)


_NN = (((2,), (1,)), ((0,), (0,)))
_NT = (((2,), (2,)), ((0,), (0,)))
_TN = (((1,), (1,)), ((0,), (0,)))


def _wkv_kernel(r_ref, k_ref, v_ref, kk_ref, b_ref, lw_ref, s0_ref, tri_ref, y_ref, sout_ref, s_scr,
                *, chunks_per_seq, prompt_chunks):
    step = pl.program_id(0)
    in_prompt = step < prompt_chunks
    cidx = step % chunks_per_seq
    first = jnp.logical_or(jnp.logical_not(in_prompt), cidx == 0)
    last = jnp.logical_or(jnp.logical_not(in_prompt), cidx == chunks_per_seq - 1)

    @pl.when(first)
    def _():
        s_scr[...] = s0_ref[0]

    npair = r_ref.shape[0]
    c = r_ref.shape[1]
    lw = lw_ref[...]
    tri = jnp.broadcast_to(tri_ref[...][None], (npair, c, c))
    lw_hi = lw.astype(BF16)
    lw_lo = (lw - lw_hi.astype(F32)).astype(BF16)
    cum = (lax.dot_general(tri, lw_hi, _NN, preferred_element_type=F32)
           + lax.dot_general(tri, lw_lo, _NN, preferred_element_type=F32))
    cum_end = cum[:, c - 1:c, :]
    e_inc = jnp.exp(cum)
    e_exc = jnp.exp(cum - lw)
    e_neg = jnp.exp(-cum)
    e_end = jnp.exp(cum_end - cum)
    p_end = jnp.exp(cum_end)

    lane = lax.broadcasted_iota(jnp.int32, (npair, c, LANES), 2)
    low = lane < HEAD_B

    def stack(z):
        return jnp.concatenate([jnp.where(low, z, 0.0), jnp.where(low, 0.0, z)], axis=1)

    kk = kk_ref[...]
    bb = b_ref[...]
    kmod = k_ref[...]
    a_t = stack(-kk * e_exc)
    r_t = stack(r_ref[...] * e_inc)
    b_t = stack(bb * e_neg)
    k_t = stack(kmod * e_neg)
    b_h = stack(bb * e_end)
    k_h = stack(kmod * e_end)
    v_s = stack(v_ref[...])

    ar = jnp.concatenate([a_t, r_t], axis=1)
    bk = jnp.concatenate([b_t, k_t], axis=1)
    gmat = _bdot(ar, bk, _NT)
    n2 = 2 * c
    ri = lax.broadcasted_iota(jnp.int32, (npair, n2, n2), 1) & (c - 1)
    ci = lax.broadcasted_iota(jnp.int32, (npair, n2, n2), 2) & (c - 1)
    strict = ri > ci
    incl = ri >= ci
    l_m = jnp.where(strict, gmat[:, :n2, :n2], 0.0)
    m_m = jnp.where(strict, gmat[:, :n2, n2:], 0.0)
    rb = jnp.where(incl, gmat[:, n2:, :n2], 0.0)
    rk = jnp.where(incl, gmat[:, n2:, n2:], 0.0)

    s_prev = s_scr[...]
    sa = _bdot(ar, s_prev, _NT)
    w_m = sa[:, :n2] + _bdot(m_m, v_s, _NN)

    eye = (lax.broadcasted_iota(jnp.int32, (npair, n2, n2), 1)
           == lax.broadcasted_iota(jnp.int32, (npair, n2, n2), 2)).astype(F32)
    t_m = eye + l_m
    pw = _bdot(l_m, l_m, _NN)
    span = 2
    while span < c:
        t_m = t_m + _bdot(t_m, pw, _NN)
        span *= 2
        if span < c:
            pw = _bdot(pw, pw, _NN)
    u_s = _bdot(t_m, w_m, _NN)

    y2 = sa[:, n2:] + _bdot(jnp.concatenate([rb, rk], axis=2), jnp.concatenate([u_s, v_s], axis=1), _NN)
    y_ref[...] = y2[:, :c] + y2[:, c:]

    s_new = s_prev * p_end + _bdot(jnp.concatenate([u_s, v_s], axis=1),
                                   jnp.concatenate([b_h, k_h], axis=1), _TN)
    s_scr[...] = s_new

    @pl.when(last)
    def _():
        sout_ref[0] = s_new


def _wkv(r, k, v, kk, b, lw, s0_pairs, n_prompt, seq):
    npair, n, _ = r.shape
    c = CHUNK
    chunks_per_seq = seq // c
    prompt_chunks = n_prompt // c
    nseq = s0_pairs.shape[0]
    n_prompt_seq = n_prompt // seq
    tri = jnp.tril(jnp.ones((c, c), F32)).astype(BF16)

    def seq_map(s):
        return (jnp.where(s < prompt_chunks, s // chunks_per_seq, n_prompt_seq + s - prompt_chunks), 0, 0, 0)

    tok_spec = pl.BlockSpec((npair, c, LANES), lambda s: (0, s, 0))
    state_spec = pl.BlockSpec((1, npair, LANES, LANES), seq_map)
    return pl.pallas_call(
        functools.partial(_wkv_kernel, chunks_per_seq=chunks_per_seq, prompt_chunks=prompt_chunks),
        grid=(n // c,),
        in_specs=[tok_spec] * 6 + [state_spec, pl.BlockSpec((c, c), lambda s: (0, 0))],
        out_specs=[tok_spec, state_spec],
        out_shape=[jax.ShapeDtypeStruct((npair, n, LANES), F32),
                   jax.ShapeDtypeStruct((nseq, npair, LANES, LANES), F32)],
        scratch_shapes=[pltpu.VMEM((npair, LANES, LANES), F32)],
        compiler_params=_cparams(("arbitrary",)),
        name="wkv_chunked",
    )(r, k, v, kk, b, lw, s0_pairs, tri)


def _state_to_pairs(s):
    nseq, h, hv, hk = s.shape
    s = s.reshape(nseq, h // 2, 2, hv, hk)
    z = jnp.zeros_like(s[:, :, 0])
    top = jnp.concatenate([s[:, :, 0], z], axis=-1)
    bot = jnp.concatenate([z, s[:, :, 1]], axis=-1)
    return jnp.concatenate([top, bot], axis=-2)


def _pairs_to_state(sp):
    nseq, npair = sp.shape[:2]
    s0 = sp[:, :, :HEAD_B, :HEAD_B]
    s1 = sp[:, :, HEAD_B:, HEAD_B:]
    return jnp.stack([s0, s1], axis=2).reshape(nseq, 2 * npair, HEAD_B, HEAD_B)


def _rwkv_out_kernel(y_ref, r_ref, k_ref, v_ref, g_ref, x_ref, lg_ref, lb_ref, rk_ref, ones_ref,
                     wo_ref, ng_ref, nb_ref, o_ref, t_ref):
    ones_blk = ones_ref[...]
    inv = 1.0 / HEAD_B
    parts = []
    for p in range(y_ref.shape[0]):
        sl = slice(p * LANES, (p + 1) * LANES)
        y = y_ref[p]
        mean = _head_sum(y, ones_blk) * inv
        yc = y - mean
        var = _head_sum(yc * yc, ones_blk) * inv
        yn = yc * lax.rsqrt(var + GN_EPS) * lg_ref[:, sl] + lb_ref[:, sl]
        bonus = _head_sum(r_ref[p] * k_ref[p] * rk_ref[:, sl], ones_blk) * v_ref[p]
        parts.append(((yn + bonus) * g_ref[:, sl]).astype(BF16))
    z = jnp.concatenate(parts, axis=1)
    h = _dot(z, wo_ref[...])
    out = _layer_norm(ALPHA * x_ref[...] + h, ng_ref[...], nb_ref[...])
    o_ref[...] = out
    _store_token_tiles(t_ref, out)


def _rwkv_out(y, r, k, v, g, x, prm, ln_g, ln_b, tm):
    n, d = x.shape
    npair = d // LANES
    row = lambda i: (i, 0)
    const2 = lambda i: (0, 0)
    pair_spec = pl.BlockSpec((npair, tm, LANES), lambda i: (0, i, 0))
    vec = pl.BlockSpec((1, d), const2)
    return pl.pallas_call(
        _rwkv_out_kernel,
        grid=(n // tm,),
        in_specs=[pair_spec] * 4 + [pl.BlockSpec((tm, d), row), pl.BlockSpec((tm, d), row),
                                    vec, vec, vec, pl.BlockSpec((LANES, LANES), const2),
                                    pl.BlockSpec((d, d), const2), vec, vec],
        out_specs=[pl.BlockSpec((tm, d), row), pl.BlockSpec((tm, npair, LANES), lambda i: (i, 0, 0))],
        out_shape=[jax.ShapeDtypeStruct((n, d), F32), jax.ShapeDtypeStruct((n, npair, LANES), F32)],
        compiler_params=_cparams(("parallel",)),
        name="rwkv_out",
    )(y, r, k, v, g, x, prm["lnx_g"], prm["lnx_b"], prm["r_k"], prm["ones"], prm["wo"], ln_g, ln_b)


def _pad_cols(w, mult):
    pad = (-w.shape[1]) % mult
    return jnp.pad(w, ((0, 0), (0, pad))) if pad else w


def _pad_rows(w, mult):
    pad = (-w.shape[0]) % mult
    return jnp.pad(w, ((0, pad), (0, 0))) if pad else w


def _rwkv_params(j, mu_b, w_rkv_b, w0_b, w1_b, w2_b, a0_b, a1_b, a2_b, v0_b, v1_b, v2_b, g1_b, g2_b,
                 k_k_b, k_a_b, r_k_b, lnx_g_b, lnx_b_b, w_o_b):
    d = mu_b.shape[-1]
    lane = jnp.arange(LANES)
    ones = (lane[:, None] // HEAD_B == lane[None, :] // HEAD_B).astype(BF16)
    prm = dict(
        mu=mu_b[j], wr=w_rkv_b[j, 0].astype(BF16), wk=w_rkv_b[j, 1].astype(BF16), wv=w_rkv_b[j, 2].astype(BF16),
        w0=w0_b[j].reshape(1, d), w1=_pad_cols(w1_b[j], LANES).astype(BF16), w2=_pad_rows(w2_b[j], LANES).astype(BF16),
        a0=a0_b[j].reshape(1, d), a1=_pad_cols(a1_b[j], LANES).astype(BF16), a2=_pad_rows(a2_b[j], LANES).astype(BF16),
        g1=_pad_cols(g1_b[j], LANES).astype(BF16), g2=_pad_rows(g2_b[j], LANES).astype(BF16),
        k_k=k_k_b[j].reshape(1, d), k_a=k_a_b[j].reshape(1, d), r_k=r_k_b[j].reshape(1, d),
        lnx_g=lnx_g_b[j].reshape(1, d), lnx_b=lnx_b_b[j].reshape(1, d), wo=w_o_b[j].astype(BF16), ones=ones,
    )
    if j > 0:
        prm.update(v0=v0_b[j - 1].reshape(1, d), v1=_pad_cols(v1_b[j - 1], LANES).astype(BF16),
                   v2=_pad_rows(v2_b[j - 1], LANES).astype(BF16))
    return prm


def kernel(x_prompt, x_sample, cache_k_a, cache_v_a, state_shift_b, state_wkv_b, w_qkv_a, sinks_a, w_o_a, mu_b, w_rkv_b, w0_b, w1_b, w2_b, a0_b, a1_b, a2_b, v0_b, v1_b, v2_b, g1_b, g2_b, k_k_b, k_a_b, r_k_b, lnx_g_b, lnx_b_b, w_o_b, ln_g, ln_b, w_rg, b_rg, w_re, b_re, w_exp_in, w_exp_out):
    bp, seq, d = x_prompt.shape
    bs, dec_seq, _ = x_sample.shape
    n_p, n_s = bp * seq, bs * dec_seq
    n = n_p + n_s
    nk = N_KV_HEADS * HEAD_DIM
    depth = ln_g.shape[0]

    x = jnp.concatenate([x_prompt.reshape(n_p, d), x_sample.reshape(n_s, d)], axis=0)

    tm_qkv = _pick_tile((512, 256, 128, 64), divides=(seq, n_s), multiple_of=(dec_seq,))
    tm_ln = _pick_tile((512, 256, 128, 64), divides=(n,))
    tm_rw = _pick_tile((256, 128, 64), divides=(n,))
    qb = _pick_tile((256, 128), divides=(seq,))
    cos_t, sin_t = _rope_tables(seq, dec_seq, tm_qkv)

    kp_l, vp_l, ks_l, vs_l = [], [], [], []
    shp_l, stp_l, shs_l, sts_l = [], [], [], []
    vfirst = None
    for i in range(depth):
        j = i // 2
        if i % 2 == 0:
            q, k, v = _qkv_rope(x, w_qkv_a[j].astype(BF16), cos_t, sin_t, tm_qkv, n_p, seq)
            op = _attention(q, k, k, v, v, sinks_a[j], bp, seq, qb, 0, True, True)
            os_ = _attention(q, cache_k_a[j].reshape(bs * WINDOW_ROWS, nk), k,
                             cache_v_a[j].reshape(bs * WINDOW_ROWS, nk), v, sinks_a[j],
                             bs, dec_seq, dec_seq, n_p, False, False)
            x, xt = _proj_ln(op, os_, w_o_a[j].astype(BF16), x, ln_g[i, 0].reshape(1, d),
                             ln_b[i, 0].reshape(1, d), tm_ln)
            keep = min(WINDOW_ROWS, seq)
            kp_l.append(k[:n_p].reshape(bp, seq, nk)[:, seq - keep:].reshape(bp, keep, N_KV_HEADS, HEAD_DIM))
            vp_l.append(v[:n_p].reshape(bp, seq, nk)[:, seq - keep:].reshape(bp, keep, N_KV_HEADS, HEAD_DIM))
            ks_l.append(k[n_p:].reshape(bs, dec_seq, N_KV_HEADS, HEAD_DIM))
            vs_l.append(v[n_p:].reshape(bs, dec_seq, N_KV_HEADS, HEAD_DIM))
        else:
            prm = _rwkv_params(j, mu_b, w_rkv_b, w0_b, w1_b, w2_b, a0_b, a1_b, a2_b, v0_b, v1_b, v2_b,
                               g1_b, g2_b, k_k_b, k_a_b, r_k_b, lnx_g_b, lnx_b_b, w_o_b)
            chunk_last = x.reshape(n // CHUNK, CHUNK, d)[:, CHUNK - 1]
            prev_p = jnp.concatenate([jnp.zeros((1, d), F32), chunk_last[:n_p // CHUNK - 1]], axis=0)
            prev_p = jnp.where((jnp.arange(n_p // CHUNK) % (seq // CHUNK) == 0)[:, None], 0.0, prev_p)
            prev_s = jnp.where((jnp.arange(n_s // CHUNK) % (dec_seq // CHUNK) == 0)[:, None],
                               jnp.repeat(state_shift_b[j], dec_seq // CHUNK, axis=0),
                               chunk_last[n_p // CHUNK - 1:n // CHUNK - 1])
            chunk_prev = jnp.concatenate([prev_p, prev_s], axis=0).reshape(n // tm_rw, tm_rw // CHUNK, d)
            r, k, v, kk, b, lw, g = _rwkv_proj(x, chunk_prev, prm, vfirst, tm_rw)
            if vfirst is None:
                vfirst = v
            s0 = jnp.concatenate([jnp.zeros((bp,) + state_wkv_b.shape[2:], F32), state_wkv_b[j]], axis=0)
            y, s_fin = _wkv(r, k, v, kk, b, lw, _state_to_pairs(s0), n_p, seq)
            s_fin = _pairs_to_state(s_fin)
            seq_last = chunk_last.reshape(-1, 1, d)
            shp_l.append(seq_last[seq // CHUNK - 1:n_p // CHUNK:seq // CHUNK, 0])
            shs_l.append(seq_last[n_p // CHUNK + dec_seq // CHUNK - 1::dec_seq // CHUNK, 0])
            stp_l.append(s_fin[:bp])
            sts_l.append(s_fin[bp:])
            x, xt = _rwkv_out(y, r, k, v, g, x, prm, ln_g[i, 0].reshape(1, d), ln_b[i, 0].reshape(1, d), tm_rw)
        w_router = jnp.pad(jnp.concatenate([w_re[i], w_rg[i]], axis=1),
                           ((0, 0), (0, LANES - N_EXPERTS - N_GROUPS))).astype(BF16)
        b_router = jnp.pad(jnp.concatenate([b_re[i], b_rg[i]]), (0, LANES - N_EXPERTS - N_GROUPS)).reshape(1, LANES)
        x = _hier_moe_ln(x, xt, w_router, b_router, w_exp_in[i].astype(BF16), w_exp_out[i].astype(BF16),
                         ln_g[i, 1].reshape(1, d), ln_b[i, 1].reshape(1, d))

    return (x[:n_p].reshape(bp, seq, d), x[n_p:].reshape(bs, dec_seq, d),
            jnp.stack(kp_l), jnp.stack(vp_l), jnp.stack(shp_l), jnp.stack(stp_l),
            jnp.stack(ks_l), jnp.stack(vs_l), jnp.stack(shs_l), jnp.stack(sts_l))
```

```python
import functools

import jax
import jax.numpy as jnp
from jax import lax
from jax.experimental import pallas as pl
from jax.experimental.pallas import tpu as pltpu

F32 = jnp.float32
BF16 = jnp.bfloat16

CHUNK = 64
HEAD_DIM = 64
N_Q_HEADS = 16
N_KV_HEADS = 4
WINDOW_ROWS = 128
ROPE_THETA = 10000.0
PAST_LEN = 2048
HEAD_B = 64
GN_EPS = 64e-5
N_GROUPS = 4
E_PER_GROUP = 8
N_EXPERTS = 32
D_EXPERT = 512
LN_EPS = 1e-5
DEPTH = 4
ALPHA = (2 * DEPTH) ** 0.25

LANES = 128
VMEM_LIMIT = 56 * 1024 * 1024
NEG_BIG = -1e30


def _cparams(sem):
    return pltpu.CompilerParams(dimension_semantics=sem, vmem_limit_bytes=VMEM_LIMIT)


def _pick_tile(cands, divides=(), multiple_of=()):
    for t in cands:
        if all(n % t == 0 for n in divides) and all(t % m == 0 for m in multiple_of):
            return t
    raise ValueError(f"no tile in {cands} for {divides} / {multiple_of}")


def _layer_norm(z, g, b):
    mu = jnp.mean(z, axis=-1, keepdims=True)
    zc = z - mu
    var = jnp.mean(zc * zc, axis=-1, keepdims=True)
    return zc * lax.rsqrt(var + LN_EPS) * g + b


def _dot(a, b):
    return jnp.dot(a, b, preferred_element_type=F32)


def _sigmoid(z):
    return 1.0 / (1.0 + jnp.exp(-z))


def _qkv_kernel(x_ref, w_ref, cos_ref, sin_ref, q_ref, k_ref, v_ref):
    nq = q_ref.shape[1]
    nk = k_ref.shape[1]
    acc = _dot(x_ref[...].astype(BF16), w_ref[...])
    cos = cos_ref[...]
    sin = sin_ref[...]
    lane = lax.broadcasted_iota(jnp.int32, cos.shape, 1)
    first_half = (lane & (HEAD_DIM - 1)) < HEAD_DIM // 2

    def rope(xg):
        rot = jnp.where(first_half, pltpu.roll(xg, LANES - HEAD_DIM // 2, 1),
                        pltpu.roll(xg, HEAD_DIM // 2, 1))
        return xg * cos + rot * sin

    for g in range(nq // LANES):
        sl = slice(g * LANES, (g + 1) * LANES)
        q_ref[:, sl] = rope(acc[:, sl]).astype(q_ref.dtype)
    for g in range(nk // LANES):
        k_ref[:, g * LANES:(g + 1) * LANES] = rope(acc[:, nq + g * LANES:nq + (g + 1) * LANES])
    v_ref[...] = acc[:, nq + nk:]


def _qkv_rope(x, w_bf, cos_t, sin_t, tm, n_prompt, seq):
    n, d = x.shape
    nq, nk = N_Q_HEADS * HEAD_DIM, N_KV_HEADS * HEAD_DIM
    tiles_per_seq = seq // tm
    prompt_tiles = n_prompt // tm

    def tab_map(i):
        return (jnp.where(i < prompt_tiles, i % tiles_per_seq, tiles_per_seq), 0)

    return pl.pallas_call(
        _qkv_kernel,
        grid=(n // tm,),
        in_specs=[
            pl.BlockSpec((tm, d), lambda i: (i, 0)),
            pl.BlockSpec((d, nq + 2 * nk), lambda i: (0, 0)),
            pl.BlockSpec((tm, LANES), tab_map),
            pl.BlockSpec((tm, LANES), tab_map),
        ],
        out_specs=[
            pl.BlockSpec((tm, nq), lambda i: (i, 0)),
            pl.BlockSpec((tm, nk), lambda i: (i, 0)),
            pl.BlockSpec((tm, nk), lambda i: (i, 0)),
        ],
        out_shape=[
            jax.ShapeDtypeStruct((n, nq), BF16),
            jax.ShapeDtypeStruct((n, nk), F32),
            jax.ShapeDtypeStruct((n, nk), F32),
        ],
        compiler_params=_cparams(("parallel",)),
        name="qkv_rope",
    )(x, w_bf, cos_t, sin_t)


def _rope_tables(seq, dec_seq, tm):
    inv = ROPE_THETA ** (-(jnp.arange(0, HEAD_DIM, 2, dtype=F32) / HEAD_DIM))
    pos = jnp.concatenate([jnp.arange(seq), PAST_LEN + (jnp.arange(tm) % dec_seq)]).astype(F32)
    ang = pos[:, None] * inv[None, :]
    cos, sin = jnp.cos(ang), jnp.sin(ang)
    reps = LANES // HEAD_DIM
    cos_t = jnp.tile(jnp.concatenate([cos, cos], axis=1), (1, reps))
    sin_t = jnp.tile(jnp.concatenate([-sin, sin], axis=1), (1, reps))
    return cos_t, sin_t


def _attn_kernel(sinks_ref, q_ref, kh_ref, km_ref, vh_ref, vm_ref, o_ref, *, qb, mask_halo):
    blk = pl.program_id(1)
    scale = HEAD_DIM ** -0.5
    gq = N_Q_HEADS // N_KV_HEADS
    kall = jnp.concatenate([kh_ref[...], km_ref[...]], axis=0)
    vall = jnp.concatenate([vh_ref[...], vm_ref[...]], axis=0)
    rows = kall.shape[0]
    low_kv = lax.broadcasted_iota(jnp.int32, (rows, LANES), 1) < HEAD_DIM
    low_q = lax.broadcasted_iota(jnp.int32, (CHUNK, LANES), 1) < HEAD_DIM
    band = WINDOW_ROWS + CHUNK
    key_row = lax.broadcasted_iota(jnp.int32, (gq * CHUNK, band), 1)
    head_row = lax.broadcasted_iota(jnp.int32, (gq * CHUNK, 1), 0) >> (CHUNK.bit_length() - 1)

    both = []
    for kvh in range(N_KV_HEADS):
        sl = slice(LANES * (kvh // 2), LANES * (kvh // 2 + 1))
        own = low_kv if kvh % 2 == 0 else jnp.logical_not(low_kv)
        per = []
        for arr in (kall, vall):
            a_own = jnp.where(own, arr[:, sl], 0.0)
            per.append((a_own + pltpu.roll(a_own, HEAD_DIM, 1)).astype(BF16))
        both.append(per)

    for kvh in range(N_KV_HEADS):
        k2, v2 = both[kvh]
        sink = jnp.zeros((gq * CHUNK, 1), F32)
        for h in range(gq):
            sink = jnp.where(head_row == h, sinks_ref[gq * kvh + h], sink)
        for c in range(qb // CHUNK):
            r0 = c * CHUNK
            parts = []
            for pair in range(gq // 2):
                col = LANES * (2 * kvh + pair)
                q2 = q_ref[r0:r0 + CHUNK, col:col + LANES]
                parts.append(jnp.where(low_q, q2, jnp.zeros_like(q2)))
                parts.append(jnp.where(low_q, jnp.zeros_like(q2), q2))
            q4 = jnp.concatenate(parts, axis=0)
            s = lax.dot_general(q4, k2[r0:r0 + band], (((1,), (1,)), ((), ())),
                                preferred_element_type=F32) * scale
            if mask_halo and r0 < WINDOW_ROWS:
                s = jnp.where(key_row >= jnp.where(blk > 0, 0, WINDOW_ROWS - r0), s, NEG_BIG)
            m = jnp.maximum(jnp.max(s, axis=-1, keepdims=True), sink)
            p = jnp.exp(s - m)
            p = p * (1.0 / (jnp.sum(p, axis=-1, keepdims=True) + jnp.exp(sink - m)))
            o4 = _dot(p.astype(BF16), v2[r0:r0 + band])
            for pair in range(gq // 2):
                col = LANES * (2 * kvh + pair)
                lo = o4[(2 * pair) * CHUNK:(2 * pair + 1) * CHUNK]
                hi = o4[(2 * pair + 1) * CHUNK:(2 * pair + 2) * CHUNK]
                o_ref[r0:r0 + CHUNK, col:col + LANES] = jnp.where(low_q, lo, hi).astype(o_ref.dtype)


def _attention(q, k_halo, k_main, v_halo, v_main, sinks, n_seq, seq, qb, row0, mask_halo, halo_from_main):
    nq = q.shape[1]
    nk = k_main.shape[1]
    blocks = seq // qb
    base = row0 // qb
    hb = qb // WINDOW_ROWS

    def main_map(bi, j):
        return (base + bi * blocks + j, 0)

    def halo_map(bi, j):
        if halo_from_main:
            return (jnp.maximum((base + bi * blocks + j) * hb - 1, 0), 0)
        return (bi, 0)

    return pl.pallas_call(
        functools.partial(_attn_kernel, qb=qb, mask_halo=mask_halo),
        grid=(n_seq, blocks),
        in_specs=[
            pl.BlockSpec(memory_space=pltpu.SMEM),
            pl.BlockSpec((qb, nq), main_map),
            pl.BlockSpec((WINDOW_ROWS, nk), halo_map),
            pl.BlockSpec((qb, nk), main_map),
            pl.BlockSpec((WINDOW_ROWS, nk), halo_map),
            pl.BlockSpec((qb, nk), main_map),
        ],
        out_specs=pl.BlockSpec((qb, nq), lambda bi, j: (bi * blocks + j, 0)),
        out_shape=jax.ShapeDtypeStruct((n_seq * seq, nq), BF16),
        compiler_params=_cparams(("parallel", "parallel")),
        name="attn_prompt" if mask_halo else "attn_sample",
    )(sinks, q, k_halo, k_main, v_halo, v_main)


def _proj_ln_kernel(ap_ref, as_ref, w_ref, x_ref, g_ref, b_ref, o_ref, *, prompt_tiles):
    a = jnp.where(pl.program_id(0) < prompt_tiles, ap_ref[...], as_ref[...])
    h = _dot(a, w_ref[...])
    o_ref[...] = _layer_norm(ALPHA * x_ref[...] + h, g_ref[...], b_ref[...])


def _proj_ln(a_prompt, a_sample, w_bf, x, g, b, tm):
    n, d = x.shape
    ka = a_prompt.shape[1]
    pt = a_prompt.shape[0] // tm
    st = a_sample.shape[0] // tm
    return pl.pallas_call(
        functools.partial(_proj_ln_kernel, prompt_tiles=pt),
        grid=(n // tm,),
        in_specs=[
            pl.BlockSpec((tm, ka), lambda i: (jnp.minimum(i, pt - 1), 0)),
            pl.BlockSpec((tm, ka), lambda i: (jnp.clip(i - pt, 0, st - 1), 0)),
            pl.BlockSpec((ka, d), lambda i: (0, 0)),
            pl.BlockSpec((tm, d), lambda i: (i, 0)),
            pl.BlockSpec((1, d), lambda i: (0, 0)),
            pl.BlockSpec((1, d), lambda i: (0, 0)),
        ],
        out_specs=pl.BlockSpec((tm, d), lambda i: (i, 0)),
        out_shape=jax.ShapeDtypeStruct((n, d), F32),
        compiler_params=_cparams(("parallel",)),
        name="proj_ln",
    )(a_prompt, a_sample, w_bf, x, g, b)


GROUP_LANE0 = N_EXPERTS


def _router_kernel(x_ref, w_ref, b_ref, o_ref):
    lg = _dot(x_ref[...].astype(BF16), w_ref[...]) + b_ref[...]
    lane = lax.broadcasted_iota(jnp.int32, lg.shape, 1)
    lane_f = lane.astype(F32)
    is_grp = jnp.logical_and(lane >= GROUP_LANE0, lane < GROUP_LANE0 + N_GROUPS)

    def top1(mask):
        m = jnp.max(jnp.where(mask, lg, NEG_BIG), axis=-1, keepdims=True)
        idx = jnp.min(jnp.where(jnp.logical_and(mask, lg == m), lane_f, 1e9), axis=-1, keepdims=True)
        return m, idx

    gm, gidx = top1(is_grp)
    grp = gidx - GROUP_LANE0
    pg = 1.0 / jnp.sum(jnp.where(is_grp, jnp.exp(lg - gm), 0.0), axis=-1, keepdims=True)
    in_grp = jnp.logical_and(lane < N_EXPERTS, (lane >> 3).astype(F32) == grp)
    m1, i1 = top1(in_grp)
    m2, i2 = top1(jnp.logical_and(in_grp, lane_f != i1))
    t = jnp.exp(m2 - m1)
    s1 = 1.0 / (1.0 + t)
    s2 = t / (1.0 + t)
    out = jnp.where(lane == 0, i1, jnp.where(lane == 1, i2,
          jnp.where(lane == 2, pg * s1, jnp.where(lane == 3, pg * s2, 0.0))))
    o_ref[...] = out


def _router(x, w_bf, bias, tm):
    n, d = x.shape
    return pl.pallas_call(
        _router_kernel,
        grid=(n // tm,),
        in_specs=[
            pl.BlockSpec((tm, d), lambda i: (i, 0)),
            pl.BlockSpec((d, LANES), lambda i: (0, 0)),
            pl.BlockSpec((1, LANES), lambda i: (0, 0)),
        ],
        out_specs=pl.BlockSpec((tm, LANES), lambda i: (i, 0)),
        out_shape=jax.ShapeDtypeStruct((n, LANES), F32),
        compiler_params=_cparams(("parallel",)),
        name="moe_router",
    )(x, w_bf, bias)


SUBLANES = 8


def _row_copy(src_hbm, dst, sem, src_row, group, sub):
    return pltpu.make_async_copy(src_hbm.at[pl.ds(src_row, 1)], dst.at[group, pl.ds(sub, 1)], sem)


def _gather_rows(idx_ref, src_hbm, dst, sem, n_rows):
    def body(g, carry):
        for sub in range(SUBLANES):
            _row_copy(src_hbm, dst, sem, idx_ref[0, 0, g * SUBLANES + sub], g, sub).start()
        return carry
    lax.fori_loop(0, n_rows // SUBLANES, body, 0)


def _wait_rows(src_hbm, dst, sem, n_rows):
    def body(g, carry):
        for sub in range(SUBLANES):
            _row_copy(src_hbm, dst, sem, 0, g, sub).wait()
        return carry
    lax.fori_loop(0, n_rows // SUBLANES, body, 0)


def _expert_kernel(blk_e_ref, nvalid_ref, tok_ref, tok_next_ref, x_hbm, win_ref, wout_ref,
                   ys_ref, buf, sem):
    i = pl.program_id(0)
    nb = pl.num_programs(0)
    bm = buf.shape[1] * SUBLANES
    slot = i % 2
    nvalid = nvalid_ref[0]

    @pl.when(jnp.logical_and(i == 0, nvalid > 0))
    def _():
        _gather_rows(tok_ref, x_hbm, buf.at[0], sem.at[0], bm)

    @pl.when(jnp.logical_and(i + 1 < nb, i + 1 < nvalid))
    def _():
        nslot = (i + 1) % 2
        _gather_rows(tok_next_ref, x_hbm, buf.at[nslot], sem.at[nslot], bm)

    @pl.when(i < nvalid)
    def _():
        _wait_rows(x_hbm, buf.at[slot], sem.at[slot], bm)
        xb = buf[slot].reshape(bm, buf.shape[3]).astype(BF16)
        h = _dot(xb, win_ref[0])
        h1 = h[:, :D_EXPERT]
        act = h1 * _sigmoid(h1) * h[:, D_EXPERT:]
        ys_ref[...] = _dot(act.astype(BF16), wout_ref[0])

    @pl.when(i >= nvalid)
    def _():
        ys_ref[...] = jnp.zeros_like(ys_ref)


def _experts(x, row_tok, blk_e, nvalid, w_in_bf, w_out_bf, bm):
    n, d = x.shape
    nb = row_tok.shape[0]
    grid_spec = pltpu.PrefetchScalarGridSpec(
        num_scalar_prefetch=2,
        grid=(nb,),
        in_specs=[
            pl.BlockSpec((1, 1, bm), lambda i, be, nv: (i, 0, 0), memory_space=pltpu.SMEM),
            pl.BlockSpec((1, 1, bm), lambda i, be, nv: (jnp.minimum(i + 1, nb - 1), 0, 0),
                         memory_space=pltpu.SMEM),
            pl.BlockSpec(memory_space=pl.ANY),
            pl.BlockSpec((1, d, 2 * D_EXPERT), lambda i, be, nv: (be[i], 0, 0)),
            pl.BlockSpec((1, D_EXPERT, d), lambda i, be, nv: (be[i], 0, 0)),
        ],
        out_specs=pl.BlockSpec((bm, d), lambda i, be, nv: (i, 0)),
        scratch_shapes=[pltpu.VMEM((2, bm // SUBLANES, SUBLANES, d), F32), pltpu.SemaphoreType.DMA((2,))],
    )
    return pl.pallas_call(
        _expert_kernel,
        grid_spec=grid_spec,
        out_shape=jax.ShapeDtypeStruct((nb * bm, d), F32),
        compiler_params=_cparams(("arbitrary",)),
        name="moe_experts",
    )(blk_e, nvalid, row_tok, row_tok, x, w_in_bf, w_out_bf)


def _combine_kernel(pos_ref, pos_next_ref, ys_hbm, route_ref, x_ref, g_ref, b_ref, o_ref, buf, sem):
    i = pl.program_id(0)
    nb = pl.num_programs(0)
    tm = x_ref.shape[0]
    slot = i % 2

    @pl.when(i == 0)
    def _():
        _gather_rows(pos_ref, ys_hbm, buf.at[0], sem.at[0], 2 * tm)

    @pl.when(i + 1 < nb)
    def _():
        nslot = (i + 1) % 2
        _gather_rows(pos_next_ref, ys_hbm, buf.at[nslot], sem.at[nslot], 2 * tm)

    _wait_rows(ys_hbm, buf.at[slot], sem.at[slot], 2 * tm)
    route = route_ref[...]
    rows = buf[slot].reshape(2 * tm, buf.shape[3])
    y = rows[:tm] * route[:, 2:3] + rows[tm:] * route[:, 3:4]
    o_ref[...] = _layer_norm(ALPHA * x_ref[...] + y, g_ref[...], b_ref[...])


def _combine_ln(ys, pos_tab, route, x, g, b, tm):
    n, d = x.shape
    nb = n // tm
    return pl.pallas_call(
        _combine_kernel,
        grid=(nb,),
        in_specs=[
            pl.BlockSpec((1, 1, 2 * tm), lambda i: (i, 0, 0), memory_space=pltpu.SMEM),
            pl.BlockSpec((1, 1, 2 * tm), lambda i: (jnp.minimum(i + 1, nb - 1), 0, 0),
                         memory_space=pltpu.SMEM),
            pl.BlockSpec(memory_space=pl.ANY),
            pl.BlockSpec((tm, LANES), lambda i: (i, 0)),
            pl.BlockSpec((tm, d), lambda i: (i, 0)),
            pl.BlockSpec((1, d), lambda i: (0, 0)),
            pl.BlockSpec((1, d), lambda i: (0, 0)),
        ],
        out_specs=pl.BlockSpec((tm, d), lambda i: (i, 0)),
        out_shape=jax.ShapeDtypeStruct((n, d), F32),
        scratch_shapes=[pltpu.VMEM((2, 2 * tm // SUBLANES, SUBLANES, d), F32), pltpu.SemaphoreType.DMA((2,))],
        compiler_params=_cparams(("arbitrary",)),
        name="moe_combine_ln",
    )(pos_tab, pos_tab, ys, route, x, g, b)


def _moe_plan(route, bm, tm):
    n = route.shape[0]
    eid = route[:, :2].astype(jnp.int32).reshape(-1)
    na = eid.shape[0]
    onehot = (eid[:, None] == jnp.arange(N_EXPERTS, dtype=jnp.int32)[None, :]).astype(jnp.int32)
    csum = jnp.cumsum(onehot, axis=0)
    rank = jnp.sum(csum * onehot, axis=1) - 1
    counts = csum[-1]
    padded = (counts + bm - 1) // bm * bm
    pad_end = jnp.cumsum(padded)
    pad_off = pad_end - padded
    start = jnp.cumsum(counts) - counts
    dest = pad_off[eid] + rank
    nb = -(-na // bm) + N_EXPERTS
    order = jnp.argsort(eid, stable=True).astype(jnp.int32)
    blk_row0 = jnp.arange(nb, dtype=jnp.int32) * bm
    blk_e = jnp.minimum(jnp.sum((blk_row0[:, None] >= pad_end[None, :]).astype(jnp.int32), axis=1),
                        N_EXPERTS - 1)
    blk_local0 = blk_row0 - pad_off[blk_e]
    blk_left = jnp.where(blk_row0 < pad_end[-1], counts[blk_e] - blk_local0, 0)
    j = jnp.arange(bm, dtype=jnp.int32)
    src = jnp.clip((start[blk_e] + blk_local0)[:, None] + j[None, :], 0, na - 1)
    row_tok = jnp.where(j[None, :] < blk_left[:, None], order[src] // 2, 0).astype(jnp.int32).reshape(nb, 1, bm)
    nvalid = (pad_end[-1] // bm).astype(jnp.int32).reshape(1)
    pos_tab = dest.astype(jnp.int32).reshape(n // tm, tm, 2).transpose(0, 2, 1).reshape(n // tm, 1, 2 * tm)
    return row_tok, blk_e, nvalid, pos_tab


def _hier_moe_ln(x, w_router_bf, b_router, w_in_bf, w_out_bf, g, b):
    n = x.shape[0]
    tm_r = _pick_tile((512, 256, 128, 64), divides=(n,))
    tm_c = _pick_tile((256, 128, 64), divides=(n,))
    bm = 256 if (2 * n) % 256 == 0 else 128
    route = _router(x, w_router_bf, b_router, tm_r)
    row_tok, blk_e, nvalid, pos_tab = _moe_plan(route, bm, tm_c)
    ys = _experts(x, row_tok, blk_e, nvalid, w_in_bf, w_out_bf, bm)
    return _combine_ln(ys, pos_tab, route, x, g, b, tm_c)


def _head_sum(z, ones_blk):
    hi = z.astype(BF16)
    lo = (z - hi.astype(F32)).astype(BF16)
    return _dot(hi, ones_blk) + _dot(lo, ones_blk)


def _rwkv_proj_kernel(*refs, has_vres):
    if has_vres:
        (x_ref, xp_ref, mu_ref, wr_ref, wk_ref, wv_ref, w0_ref, w1_ref, w2_ref, a0_ref, a1_ref, a2_ref,
         g1_ref, g2_ref, kk_ref, ka_ref, ones_ref, v0_ref, v1_ref, v2_ref, vf_ref,
         r_o, k_o, v_o, kk_o, b_o, lw_o, g_o) = refs
    else:
        (x_ref, xp_ref, mu_ref, wr_ref, wk_ref, wv_ref, w0_ref, w1_ref, w2_ref, a0_ref, a1_ref, a2_ref,
         g1_ref, g2_ref, kk_ref, ka_ref, ones_ref,
         r_o, k_o, v_o, kk_o, b_o, lw_o, g_o) = refs
    x = x_ref[...]
    tm = x.shape[0]
    heads_of_chunks = jnp.concatenate(
        [jnp.broadcast_to(xp_ref[0, c:c + 1, :], (CHUNK, x.shape[1])) for c in range(tm // CHUNK)], axis=0)
    chunk_row = lax.broadcasted_iota(jnp.int32, (tm, 1), 0) & (CHUNK - 1)
    xx = jnp.where(chunk_row == 0, heads_of_chunks, pltpu.roll(x, 1, 0)) - x
    mu = mu_ref[...]

    def mix(j):
        return (x + xx * mu[j:j + 1]).astype(BF16)

    r = _dot(mix(0), wr_ref[...])
    z = w0_ref[...] + _dot(jnp.tanh(_dot(mix(1), w1_ref[...])).astype(BF16), w2_ref[...])
    wlog = -(jnp.maximum(-z, 0.0) + jnp.log(1.0 + jnp.exp(-jnp.abs(z)))) - 0.5
    lw = -jnp.exp(wlog)
    kraw = _dot(mix(2), wk_ref[...])
    xv = mix(3)
    v = _dot(xv, wv_ref[...])
    if has_vres:
        gate_v = _sigmoid(v0_ref[...] + _dot(_dot(xv, v1_ref[...]).astype(BF16), v2_ref[...]))
    a = _sigmoid(a0_ref[...] + _dot(_dot(mix(4), a1_ref[...]).astype(BF16), a2_ref[...]))
    g_o[...] = _dot(_sigmoid(_dot(mix(5), g1_ref[...])).astype(BF16), g2_ref[...])
    kkraw = kraw * kk_ref[...]
    kmod = kraw * (1.0 + (a - 1.0) * ka_ref[...])
    ones_blk = ones_ref[...]
    for p in range(r_o.shape[0]):
        sl = slice(p * LANES, (p + 1) * LANES)
        kkp = kkraw[:, sl]
        nrm = jnp.sqrt(_head_sum(kkp * kkp, ones_blk))
        kkn = kkp / jnp.maximum(nrm, 1e-12)
        vp = v[:, sl]
        if has_vres:
            vp = vp + (vf_ref[p] - vp) * gate_v[:, sl]
        r_o[p] = r[:, sl]
        k_o[p] = kmod[:, sl]
        v_o[p] = vp
        kk_o[p] = kkn
        b_o[p] = kkn * a[:, sl]
        lw_o[p] = lw[:, sl]


def _rwkv_proj(x, chunk_prev, prm, vfirst, tm):
    n, d = x.shape
    npair = d // LANES
    has_vres = vfirst is not None
    row = lambda i: (i, 0)
    const2 = lambda i: (0, 0)
    pair_spec = pl.BlockSpec((npair, tm, LANES), lambda i: (0, i, 0))

    def wspec(w):
        return pl.BlockSpec(w.shape, const2)

    args = [x, chunk_prev, prm["mu"], prm["wr"], prm["wk"], prm["wv"], prm["w0"], prm["w1"], prm["w2"],
            prm["a0"], prm["a1"], prm["a2"], prm["g1"], prm["g2"], prm["k_k"], prm["k_a"], prm["ones"]]
    in_specs = ([pl.BlockSpec((tm, d), row), pl.BlockSpec((1, tm // CHUNK, d), lambda i: (i, 0, 0))]
                + [wspec(w) for w in args[2:]])
    if has_vres:
        args += [prm["v0"], prm["v1"], prm["v2"], vfirst]
        in_specs += [wspec(prm["v0"]), wspec(prm["v1"]), wspec(prm["v2"]), pair_spec]
    pair_shape = jax.ShapeDtypeStruct((npair, n, LANES), F32)
    return pl.pallas_call(
        functools.partial(_rwkv_proj_kernel, has_vres=has_vres),
        grid=(n // tm,),
        in_specs=in_specs,
        out_specs=[pair_spec] * 6 + [pl.BlockSpec((tm, d), row)],
        out_shape=[pair_shape] * 6 + [jax.ShapeDtypeStruct((n, d), F32)],
        compiler_params=_cparams(("parallel",)),
        name="rwkv_proj",
    )(*args)


def _bdot(a, b, dims):
    return lax.dot_general(a.astype(BF16), b.astype(BF16), dims, preferred_element_type=F32)


_NN = (((2,), (1,)), ((0,), (0,)))
_NT = (((2,), (2,)), ((0,), (0,)))
_TN = (((1,), (1,)), ((0,), (0,)))


def _wkv_kernel(r_ref, k_ref, v_ref, kk_ref, b_ref, lw_ref, s0_ref, tri_ref, y_ref, sout_ref, s_scr,
                *, chunks_per_seq, prompt_chunks):
    step = pl.program_id(0)
    in_prompt = step < prompt_chunks
    cidx = step % chunks_per_seq
    first = jnp.logical_or(jnp.logical_not(in_prompt), cidx == 0)
    last = jnp.logical_or(jnp.logical_not(in_prompt), cidx == chunks_per_seq - 1)

    @pl.when(first)
    def _():
        s_scr[...] = s0_ref[0]

    npair = r_ref.shape[0]
    c = r_ref.shape[1]
    lw = lw_ref[...]
    tri = jnp.broadcast_to(tri_ref[...][None], (npair, c, c))
    lw_hi = lw.astype(BF16)
    lw_lo = (lw - lw_hi.astype(F32)).astype(BF16)
    cum = (lax.dot_general(tri, lw_hi, _NN, preferred_element_type=F32)
           + lax.dot_general(tri, lw_lo, _NN, preferred_element_type=F32))
    cum_end = cum[:, c - 1:c, :]
    e_inc = jnp.exp(cum)
    e_exc = jnp.exp(cum - lw)
    e_neg = jnp.exp(-cum)
    e_end = jnp.exp(cum_end - cum)
    p_end = jnp.exp(cum_end)

    lane = lax.broadcasted_iota(jnp.int32, (npair, c, LANES), 2)
    low = lane < HEAD_B

    def stack(z):
        return jnp.concatenate([jnp.where(low, z, 0.0), jnp.where(low, 0.0, z)], axis=1)

    kk = kk_ref[...]
    bb = b_ref[...]
    kmod = k_ref[...]
    a_t = stack(-kk * e_exc)
    r_t = stack(r_ref[...] * e_inc)
    b_t = stack(bb * e_neg)
    k_t = stack(kmod * e_neg)
    b_h = stack(bb * e_end)
    k_h = stack(kmod * e_end)
    v_s = stack(v_ref[...])

    ar = jnp.concatenate([a_t, r_t], axis=1)
    bk = jnp.concatenate([b_t, k_t], axis=1)
    gmat = _bdot(ar, bk, _NT)
    n2 = 2 * c
    ri = lax.broadcasted_iota(jnp.int32, (npair, n2, n2), 1) & (c - 1)
    ci = lax.broadcasted_iota(jnp.int32, (npair, n2, n2), 2) & (c - 1)
    strict = ri > ci
    incl = ri >= ci
    l_m = jnp.where(strict, gmat[:, :n2, :n2], 0.0)
    m_m = jnp.where(strict, gmat[:, :n2, n2:], 0.0)
    rb = jnp.where(incl, gmat[:, n2:, :n2], 0.0)
    rk = jnp.where(incl, gmat[:, n2:, n2:], 0.0)

    s_prev = s_scr[...]
    sa = _bdot(ar, s_prev, _NT)
    w_m = sa[:, :n2] + _bdot(m_m, v_s, _NN)

    eye = (lax.broadcasted_iota(jnp.int32, (npair, n2, n2), 1)
           == lax.broadcasted_iota(jnp.int32, (npair, n2, n2), 2)).astype(F32)
    t_m = eye + l_m
    pw = _bdot(l_m, l_m, _NN)
    span = 2
    while span < c:
        t_m = t_m + _bdot(t_m, pw, _NN)
        span *= 2
        if span < c:
            pw = _bdot(pw, pw, _NN)
    u_s = _bdot(t_m, w_m, _NN)

    y2 = sa[:, n2:] + _bdot(jnp.concatenate([rb, rk], axis=2), jnp.concatenate([u_s, v_s], axis=1), _NN)
    y_ref[...] = y2[:, :c] + y2[:, c:]

    s_new = s_prev * p_end + _bdot(jnp.concatenate([u_s, v_s], axis=1),
                                   jnp.concatenate([b_h, k_h], axis=1), _TN)
    s_scr[...] = s_new

    @pl.when(last)
    def _():
        sout_ref[0] = s_new


def _wkv(r, k, v, kk, b, lw, s0_pairs, n_prompt, seq):
    npair, n, _ = r.shape
    c = CHUNK
    chunks_per_seq = seq // c
    prompt_chunks = n_prompt // c
    nseq = s0_pairs.shape[0]
    n_prompt_seq = n_prompt // seq
    tri = jnp.tril(jnp.ones((c, c), F32)).astype(BF16)

    def seq_map(s):
        return (jnp.where(s < prompt_chunks, s // chunks_per_seq, n_prompt_seq + s - prompt_chunks), 0, 0, 0)

    tok_spec = pl.BlockSpec((npair, c, LANES), lambda s: (0, s, 0))
    state_spec = pl.BlockSpec((1, npair, LANES, LANES), seq_map)
    return pl.pallas_call(
        functools.partial(_wkv_kernel, chunks_per_seq=chunks_per_seq, prompt_chunks=prompt_chunks),
        grid=(n // c,),
        in_specs=[tok_spec] * 6 + [state_spec, pl.BlockSpec((c, c), lambda s: (0, 0))],
        out_specs=[tok_spec, state_spec],
        out_shape=[jax.ShapeDtypeStruct((npair, n, LANES), F32),
                   jax.ShapeDtypeStruct((nseq, npair, LANES, LANES), F32)],
        scratch_shapes=[pltpu.VMEM((npair, LANES, LANES), F32)],
        compiler_params=_cparams(("arbitrary",)),
        name="wkv_chunked",
    )(r, k, v, kk, b, lw, s0_pairs, tri)


def _state_to_pairs(s):
    nseq, h, hv, hk = s.shape
    s = s.reshape(nseq, h // 2, 2, hv, hk)
    z = jnp.zeros_like(s[:, :, 0])
    top = jnp.concatenate([s[:, :, 0], z], axis=-1)
    bot = jnp.concatenate([z, s[:, :, 1]], axis=-1)
    return jnp.concatenate([top, bot], axis=-2)


def _pairs_to_state(sp):
    nseq, npair = sp.shape[:2]
    s0 = sp[:, :, :HEAD_B, :HEAD_B]
    s1 = sp[:, :, HEAD_B:, HEAD_B:]
    return jnp.stack([s0, s1], axis=2).reshape(nseq, 2 * npair, HEAD_B, HEAD_B)


def _rwkv_out_kernel(y_ref, r_ref, k_ref, v_ref, g_ref, x_ref, lg_ref, lb_ref, rk_ref, ones_ref,
                     wo_ref, ng_ref, nb_ref, o_ref):
    ones_blk = ones_ref[...]
    inv = 1.0 / HEAD_B
    parts = []
    for p in range(y_ref.shape[0]):
        sl = slice(p * LANES, (p + 1) * LANES)
        y = y_ref[p]
        mean = _head_sum(y, ones_blk) * inv
        yc = y - mean
        var = _head_sum(yc * yc, ones_blk) * inv
        yn = yc * lax.rsqrt(var + GN_EPS) * lg_ref[:, sl] + lb_ref[:, sl]
        bonus = _head_sum(r_ref[p] * k_ref[p] * rk_ref[:, sl], ones_blk) * v_ref[p]
        parts.append(((yn + bonus) * g_ref[:, sl]).astype(BF16))
    z = jnp.concatenate(parts, axis=1)
    h = _dot(z, wo_ref[...])
    o_ref[...] = _layer_norm(ALPHA * x_ref[...] + h, ng_ref[...], nb_ref[...])


def _rwkv_out(y, r, k, v, g, x, prm, ln_g, ln_b, tm):
    n, d = x.shape
    npair = d // LANES
    row = lambda i: (i, 0)
    const2 = lambda i: (0, 0)
    pair_spec = pl.BlockSpec((npair, tm, LANES), lambda i: (0, i, 0))
    vec = pl.BlockSpec((1, d), const2)
    return pl.pallas_call(
        _rwkv_out_kernel,
        grid=(n // tm,),
        in_specs=[pair_spec] * 4 + [pl.BlockSpec((tm, d), row), pl.BlockSpec((tm, d), row),
                                    vec, vec, vec, pl.BlockSpec((LANES, LANES), const2),
                                    pl.BlockSpec((d, d), const2), vec, vec],
        out_specs=pl.BlockSpec((tm, d), row),
        out_shape=jax.ShapeDtypeStruct((n, d), F32),
        compiler_params=_cparams(("parallel",)),
        name="rwkv_out",
    )(y, r, k, v, g, x, prm["lnx_g"], prm["lnx_b"], prm["r_k"], prm["ones"], prm["wo"], ln_g, ln_b)


def _pad_cols(w, mult):
    pad = (-w.shape[1]) % mult
    return jnp.pad(w, ((0, 0), (0, pad))) if pad else w


def _pad_rows(w, mult):
    pad = (-w.shape[0]) % mult
    return jnp.pad(w, ((0, pad), (0, 0))) if pad else w


def _rwkv_params(j, mu_b, w_rkv_b, w0_b, w1_b, w2_b, a0_b, a1_b, a2_b, v0_b, v1_b, v2_b, g1_b, g2_b,
                 k_k_b, k_a_b, r_k_b, lnx_g_b, lnx_b_b, w_o_b):
    d = mu_b.shape[-1]
    lane = jnp.arange(LANES)
    ones = (lane[:, None] // HEAD_B == lane[None, :] // HEAD_B).astype(BF16)
    prm = dict(
        mu=mu_b[j], wr=w_rkv_b[j, 0].astype(BF16), wk=w_rkv_b[j, 1].astype(BF16), wv=w_rkv_b[j, 2].astype(BF16),
        w0=w0_b[j].reshape(1, d), w1=_pad_cols(w1_b[j], LANES).astype(BF16), w2=_pad_rows(w2_b[j], LANES).astype(BF16),
        a0=a0_b[j].reshape(1, d), a1=_pad_cols(a1_b[j], LANES).astype(BF16), a2=_pad_rows(a2_b[j], LANES).astype(BF16),
        g1=_pad_cols(g1_b[j], LANES).astype(BF16), g2=_pad_rows(g2_b[j], LANES).astype(BF16),
        k_k=k_k_b[j].reshape(1, d), k_a=k_a_b[j].reshape(1, d), r_k=r_k_b[j].reshape(1, d),
        lnx_g=lnx_g_b[j].reshape(1, d), lnx_b=lnx_b_b[j].reshape(1, d), wo=w_o_b[j].astype(BF16), ones=ones,
    )
    if j > 0:
        prm.update(v0=v0_b[j - 1].reshape(1, d), v1=_pad_cols(v1_b[j - 1], LANES).astype(BF16),
                   v2=_pad_rows(v2_b[j - 1], LANES).astype(BF16))
    return prm


def kernel(x_prompt, x_sample, cache_k_a, cache_v_a, state_shift_b, state_wkv_b, w_qkv_a, sinks_a, w_o_a, mu_b, w_rkv_b, w0_b, w1_b, w2_b, a0_b, a1_b, a2_b, v0_b, v1_b, v2_b, g1_b, g2_b, k_k_b, k_a_b, r_k_b, lnx_g_b, lnx_b_b, w_o_b, ln_g, ln_b, w_rg, b_rg, w_re, b_re, w_exp_in, w_exp_out):
    bp, seq, d = x_prompt.shape
    bs, dec_seq, _ = x_sample.shape
    n_p, n_s = bp * seq, bs * dec_seq
    n = n_p + n_s
    nk = N_KV_HEADS * HEAD_DIM
    depth = ln_g.shape[0]

    x = jnp.concatenate([x_prompt.reshape(n_p, d), x_sample.reshape(n_s, d)], axis=0)

    tm_qkv = _pick_tile((512, 256, 128, 64), divides=(seq, n_s), multiple_of=(dec_seq,))
    tm_ln = _pick_tile((512, 256, 128, 64), divides=(n,))
    tm_rw = _pick_tile((256, 128, 64), divides=(n,))
    qb = _pick_tile((256, 128), divides=(seq,))
    cos_t, sin_t = _rope_tables(seq, dec_seq, tm_qkv)

    kp_l, vp_l, ks_l, vs_l = [], [], [], []
    shp_l, stp_l, shs_l, sts_l = [], [], [], []
    vfirst = None
    for i in range(depth):
        j = i // 2
        if i % 2 == 0:
            q, k, v = _qkv_rope(x, w_qkv_a[j].astype(BF16), cos_t, sin_t, tm_qkv, n_p, seq)
            op = _attention(q, k, k, v, v, sinks_a[j], bp, seq, qb, 0, True, True)
            os_ = _attention(q, cache_k_a[j].reshape(bs * WINDOW_ROWS, nk), k,
                             cache_v_a[j].reshape(bs * WINDOW_ROWS, nk), v, sinks_a[j],
                             bs, dec_seq, dec_seq, n_p, False, False)
            x = _proj_ln(op, os_, w_o_a[j].astype(BF16), x, ln_g[i, 0].reshape(1, d),
                         ln_b[i, 0].reshape(1, d), tm_ln)
            keep = min(WINDOW_ROWS, seq)
            kp_l.append(k[:n_p].reshape(bp, seq, nk)[:, seq - keep:].reshape(bp, keep, N_KV_HEADS, HEAD_DIM))
            vp_l.append(v[:n_p].reshape(bp, seq, nk)[:, seq - keep:].reshape(bp, keep, N_KV_HEADS, HEAD_DIM))
            ks_l.append(k[n_p:].reshape(bs, dec_seq, N_KV_HEADS, HEAD_DIM))
            vs_l.append(v[n_p:].reshape(bs, dec_seq, N_KV_HEADS, HEAD_DIM))
        else:
            prm = _rwkv_params(j, mu_b, w_rkv_b, w0_b, w1_b, w2_b, a0_b, a1_b, a2_b, v0_b, v1_b, v2_b,
                               g1_b, g2_b, k_k_b, k_a_b, r_k_b, lnx_g_b, lnx_b_b, w_o_b)
            chunk_last = x.reshape(n // CHUNK, CHUNK, d)[:, CHUNK - 1]
            prev_p = jnp.concatenate([jnp.zeros((1, d), F32), chunk_last[:n_p // CHUNK - 1]], axis=0)
            prev_p = jnp.where((jnp.arange(n_p // CHUNK) % (seq // CHUNK) == 0)[:, None], 0.0, prev_p)
            prev_s = jnp.where((jnp.arange(n_s // CHUNK) % (dec_seq // CHUNK) == 0)[:, None],
                               jnp.repeat(state_shift_b[j], dec_seq // CHUNK, axis=0),
                               chunk_last[n_p // CHUNK - 1:n // CHUNK - 1])
            chunk_prev = jnp.concatenate([prev_p, prev_s], axis=0).reshape(n // tm_rw, tm_rw // CHUNK, d)
            r, k, v, kk, b, lw, g = _rwkv_proj(x, chunk_prev, prm, vfirst, tm_rw)
            if vfirst is None:
                vfirst = v
            s0 = jnp.concatenate([jnp.zeros((bp,) + state_wkv_b.shape[2:], F32), state_wkv_b[j]], axis=0)
            y, s_fin = _wkv(r, k, v, kk, b, lw, _state_to_pairs(s0), n_p, seq)
            s_fin = _pairs_to_state(s_fin)
            seq_last = chunk_last.reshape(-1, 1, d)
            shp_l.append(seq_last[seq // CHUNK - 1:n_p // CHUNK:seq // CHUNK, 0])
            shs_l.append(seq_last[n_p // CHUNK + dec_seq // CHUNK - 1::dec_seq // CHUNK, 0])
            stp_l.append(s_fin[:bp])
            sts_l.append(s_fin[bp:])
            x = _rwkv_out(y, r, k, v, g, x, prm, ln_g[i, 0].reshape(1, d), ln_b[i, 0].reshape(1, d), tm_rw)
        w_router = jnp.pad(jnp.concatenate([w_re[i], w_rg[i]], axis=1),
                           ((0, 0), (0, LANES - N_EXPERTS - N_GROUPS))).astype(BF16)
        b_router = jnp.pad(jnp.concatenate([b_re[i], b_rg[i]]), (0, LANES - N_EXPERTS - N_GROUPS)).reshape(1, LANES)
        x = _hier_moe_ln(x, w_router, b_router, w_exp_in[i].astype(BF16), w_exp_out[i].astype(BF16),
                         ln_g[i, 1].reshape(1, d), ln_b[i, 1].reshape(1, d))

    return (x[:n_p].reshape(bp, seq, d), x[n_p:].reshape(bs, dec_seq, d),
            jnp.stack(kp_l), jnp.stack(vp_l), jnp.stack(shp_l), jnp.stack(stp_l),
            jnp.stack(ks_l), jnp.stack(vs_l), jnp.stack(shs_l), jnp.stack(sts_l))
```

```python
import functools

import jax
import jax.numpy as jnp
from jax import lax
from jax.experimental import pallas as pl
from jax.experimental.pallas import tpu as pltpu

F32 = jnp.float32
BF16 = jnp.bfloat16

CHUNK = 64
HEAD_DIM = 64
N_Q_HEADS = 16
N_KV_HEADS = 4
WINDOW_ROWS = 128
ROPE_THETA = 10000.0
PAST_LEN = 2048
HEAD_B = 64
GN_EPS = 64e-5
N_GROUPS = 4
E_PER_GROUP = 8
N_EXPERTS = 32
D_EXPERT = 512
LN_EPS = 1e-5
DEPTH = 4
ALPHA = (2 * DEPTH) ** 0.25

LANES = 128
VMEM_LIMIT = 56 * 1024 * 1024
NEG_BIG = -1e30


def _cparams(sem):
    return pltpu.CompilerParams(dimension_semantics=sem, vmem_limit_bytes=VMEM_LIMIT)


def _pick_tile(cands, divides=(), multiple_of=()):
    for t in cands:
        if all(n % t == 0 for n in divides) and all(t % m == 0 for m in multiple_of):
            return t
    raise ValueError(f"no tile in {cands} for {divides} / {multiple_of}")


def _layer_norm(z, g, b):
    mu = jnp.mean(z, axis=-1, keepdims=True)
    zc = z - mu
    var = jnp.mean(zc * zc, axis=-1, keepdims=True)
    return zc * lax.rsqrt(var + LN_EPS) * g + b


def _dot(a, b):
    return jnp.dot(a, b, preferred_element_type=F32)


def _sigmoid(z):
    return 1.0 / (1.0 + jnp.exp(-z))


def _qkv_kernel(x_ref, w_ref, cos_ref, sin_ref, q_ref, k_ref, v_ref):
    nq = q_ref.shape[1]
    nk = k_ref.shape[1]
    acc = _dot(x_ref[...].astype(BF16), w_ref[...])
    cos = cos_ref[...]
    sin = sin_ref[...]
    lane = lax.broadcasted_iota(jnp.int32, cos.shape, 1)
    first_half = (lane & (HEAD_DIM - 1)) < HEAD_DIM // 2

    def rope(xg):
        rot = jnp.where(first_half, pltpu.roll(xg, LANES - HEAD_DIM // 2, 1),
                        pltpu.roll(xg, HEAD_DIM // 2, 1))
        return xg * cos + rot * sin

    for g in range(nq // LANES):
        sl = slice(g * LANES, (g + 1) * LANES)
        q_ref[:, sl] = rope(acc[:, sl]).astype(q_ref.dtype)
    for g in range(nk // LANES):
        k_ref[:, g * LANES:(g + 1) * LANES] = rope(acc[:, nq + g * LANES:nq + (g + 1) * LANES])
    v_ref[...] = acc[:, nq + nk:]


def _qkv_rope(x, w_bf, cos_t, sin_t, tm, n_prompt, seq):
    n, d = x.shape
    nq, nk = N_Q_HEADS * HEAD_DIM, N_KV_HEADS * HEAD_DIM
    tiles_per_seq = seq // tm
    prompt_tiles = n_prompt // tm

    def tab_map(i):
        return (jnp.where(i < prompt_tiles, i % tiles_per_seq, tiles_per_seq), 0)

    return pl.pallas_call(
        _qkv_kernel,
        grid=(n // tm,),
        in_specs=[
            pl.BlockSpec((tm, d), lambda i: (i, 0)),
            pl.BlockSpec((d, nq + 2 * nk), lambda i: (0, 0)),
            pl.BlockSpec((tm, LANES), tab_map),
            pl.BlockSpec((tm, LANES), tab_map),
        ],
        out_specs=[
            pl.BlockSpec((tm, nq), lambda i: (i, 0)),
            pl.BlockSpec((tm, nk), lambda i: (i, 0)),
            pl.BlockSpec((tm, nk), lambda i: (i, 0)),
        ],
        out_shape=[
            jax.ShapeDtypeStruct((n, nq), BF16),
            jax.ShapeDtypeStruct((n, nk), F32),
            jax.ShapeDtypeStruct((n, nk), F32),
        ],
        compiler_params=_cparams(("parallel",)),
        name="qkv_rope",
    )(x, w_bf, cos_t, sin_t)


def _rope_tables(seq, dec_seq, tm):
    inv = ROPE_THETA ** (-(jnp.arange(0, HEAD_DIM, 2, dtype=F32) / HEAD_DIM))
    pos = jnp.concatenate([jnp.arange(seq), PAST_LEN + (jnp.arange(tm) % dec_seq)]).astype(F32)
    ang = pos[:, None] * inv[None, :]
    cos, sin = jnp.cos(ang), jnp.sin(ang)
    reps = LANES // HEAD_DIM
    cos_t = jnp.tile(jnp.concatenate([cos, cos], axis=1), (1, reps))
    sin_t = jnp.tile(jnp.concatenate([-sin, sin], axis=1), (1, reps))
    return cos_t, sin_t


def _attn_kernel(sinks_ref, q_ref, kh_ref, km_ref, vh_ref, vm_ref, o_ref, *, qb, mask_halo):
    blk = pl.program_id(1)
    scale = HEAD_DIM ** -0.5
    gq = N_Q_HEADS // N_KV_HEADS
    kall = jnp.concatenate([kh_ref[...], km_ref[...]], axis=0)
    vall = jnp.concatenate([vh_ref[...], vm_ref[...]], axis=0)
    rows = kall.shape[0]
    low_kv = lax.broadcasted_iota(jnp.int32, (rows, LANES), 1) < HEAD_DIM
    low_q = lax.broadcasted_iota(jnp.int32, (CHUNK, LANES), 1) < HEAD_DIM
    band = WINDOW_ROWS + CHUNK
    key_row = lax.broadcasted_iota(jnp.int32, (gq * CHUNK, band), 1)
    head_row = lax.broadcasted_iota(jnp.int32, (gq * CHUNK, 1), 0) >> (CHUNK.bit_length() - 1)

    both = []
    for kvh in range(N_KV_HEADS):
        sl = slice(LANES * (kvh // 2), LANES * (kvh // 2 + 1))
        own = low_kv if kvh % 2 == 0 else jnp.logical_not(low_kv)
        per = []
        for arr in (kall, vall):
            a_own = jnp.where(own, arr[:, sl], 0.0)
            per.append((a_own + pltpu.roll(a_own, HEAD_DIM, 1)).astype(BF16))
        both.append(per)

    for kvh in range(N_KV_HEADS):
        k2, v2 = both[kvh]
        sink = jnp.zeros((gq * CHUNK, 1), F32)
        for h in range(gq):
            sink = jnp.where(head_row == h, sinks_ref[gq * kvh + h], sink)
        for c in range(qb // CHUNK):
            r0 = c * CHUNK
            parts = []
            for pair in range(gq // 2):
                col = LANES * (2 * kvh + pair)
                q2 = q_ref[r0:r0 + CHUNK, col:col + LANES]
                parts.append(jnp.where(low_q, q2, jnp.zeros_like(q2)))
                parts.append(jnp.where(low_q, jnp.zeros_like(q2), q2))
            q4 = jnp.concatenate(parts, axis=0)
            s = lax.dot_general(q4, k2[r0:r0 + band], (((1,), (1,)), ((), ())),
                                preferred_element_type=F32) * scale
            if mask_halo and r0 < WINDOW_ROWS:
                s = jnp.where(key_row >= jnp.where(blk > 0, 0, WINDOW_ROWS - r0), s, NEG_BIG)
            m = jnp.maximum(jnp.max(s, axis=-1, keepdims=True), sink)
            p = jnp.exp(s - m)
            p = p * (1.0 / (jnp.sum(p, axis=-1, keepdims=True) + jnp.exp(sink - m)))
            o4 = _dot(p.astype(BF16), v2[r0:r0 + band])
            for pair in range(gq // 2):
                col = LANES * (2 * kvh + pair)
                lo = o4[(2 * pair) * CHUNK:(2 * pair + 1) * CHUNK]
                hi = o4[(2 * pair + 1) * CHUNK:(2 * pair + 2) * CHUNK]
                o_ref[r0:r0 + CHUNK, col:col + LANES] = jnp.where(low_q, lo, hi).astype(o_ref.dtype)


def _attention(q, k_halo, k_main, v_halo, v_main, sinks, n_seq, seq, qb, row0, mask_halo, halo_from_main):
    nq = q.shape[1]
    nk = k_main.shape[1]
    blocks = seq // qb
    base = row0 // qb
    hb = qb // WINDOW_ROWS

    def main_map(bi, j):
        return (base + bi * blocks + j, 0)

    def halo_map(bi, j):
        if halo_from_main:
            return (jnp.maximum((base + bi * blocks + j) * hb - 1, 0), 0)
        return (bi, 0)

    return pl.pallas_call(
        functools.partial(_attn_kernel, qb=qb, mask_halo=mask_halo),
        grid=(n_seq, blocks),
        in_specs=[
            pl.BlockSpec(memory_space=pltpu.SMEM),
            pl.BlockSpec((qb, nq), main_map),
            pl.BlockSpec((WINDOW_ROWS, nk), halo_map),
            pl.BlockSpec((qb, nk), main_map),
            pl.BlockSpec((WINDOW_ROWS, nk), halo_map),
            pl.BlockSpec((qb, nk), main_map),
        ],
        out_specs=pl.BlockSpec((qb, nq), lambda bi, j: (bi * blocks + j, 0)),
        out_shape=jax.ShapeDtypeStruct((n_seq * seq, nq), BF16),
        compiler_params=_cparams(("parallel", "parallel")),
        name="attn_prompt" if mask_halo else "attn_sample",
    )(sinks, q, k_halo, k_main, v_halo, v_main)


def _proj_ln_kernel(ap_ref, as_ref, w_ref, x_ref, g_ref, b_ref, o_ref, *, prompt_tiles):
    a = jnp.where(pl.program_id(0) < prompt_tiles, ap_ref[...], as_ref[...])
    h = _dot(a, w_ref[...])
    o_ref[...] = _layer_norm(ALPHA * x_ref[...] + h, g_ref[...], b_ref[...])


def _proj_ln(a_prompt, a_sample, w_bf, x, g, b, tm):
    n, d = x.shape
    ka = a_prompt.shape[1]
    pt = a_prompt.shape[0] // tm
    st = a_sample.shape[0] // tm
    return pl.pallas_call(
        functools.partial(_proj_ln_kernel, prompt_tiles=pt),
        grid=(n // tm,),
        in_specs=[
            pl.BlockSpec((tm, ka), lambda i: (jnp.minimum(i, pt - 1), 0)),
            pl.BlockSpec((tm, ka), lambda i: (jnp.clip(i - pt, 0, st - 1), 0)),
            pl.BlockSpec((ka, d), lambda i: (0, 0)),
            pl.BlockSpec((tm, d), lambda i: (i, 0)),
            pl.BlockSpec((1, d), lambda i: (0, 0)),
            pl.BlockSpec((1, d), lambda i: (0, 0)),
        ],
        out_specs=pl.BlockSpec((tm, d), lambda i: (i, 0)),
        out_shape=jax.ShapeDtypeStruct((n, d), F32),
        compiler_params=_cparams(("parallel",)),
        name="proj_ln",
    )(a_prompt, a_sample, w_bf, x, g, b)


GROUP_LANE0 = N_EXPERTS


def _router_kernel(x_ref, w_ref, b_ref, o_ref):
    lg = _dot(x_ref[...].astype(BF16), w_ref[...]) + b_ref[...]
    lane = lax.broadcasted_iota(jnp.int32, lg.shape, 1)
    lane_f = lane.astype(F32)
    is_grp = jnp.logical_and(lane >= GROUP_LANE0, lane < GROUP_LANE0 + N_GROUPS)

    def top1(mask):
        m = jnp.max(jnp.where(mask, lg, NEG_BIG), axis=-1, keepdims=True)
        idx = jnp.min(jnp.where(jnp.logical_and(mask, lg == m), lane_f, 1e9), axis=-1, keepdims=True)
        return m, idx

    gm, gidx = top1(is_grp)
    grp = gidx - GROUP_LANE0
    pg = 1.0 / jnp.sum(jnp.where(is_grp, jnp.exp(lg - gm), 0.0), axis=-1, keepdims=True)
    in_grp = jnp.logical_and(lane < N_EXPERTS, (lane >> 3).astype(F32) == grp)
    m1, i1 = top1(in_grp)
    m2, i2 = top1(jnp.logical_and(in_grp, lane_f != i1))
    t = jnp.exp(m2 - m1)
    s1 = 1.0 / (1.0 + t)
    s2 = t / (1.0 + t)
    out = jnp.where(lane == 0, i1, jnp.where(lane == 1, i2,
          jnp.where(lane == 2, pg * s1, jnp.where(lane == 3, pg * s2, 0.0))))
    o_ref[...] = out


def _router(x, w_bf, bias, tm):
    n, d = x.shape
    return pl.pallas_call(
        _router_kernel,
        grid=(n // tm,),
        in_specs=[
            pl.BlockSpec((tm, d), lambda i: (i, 0)),
            pl.BlockSpec((d, LANES), lambda i: (0, 0)),
            pl.BlockSpec((1, LANES), lambda i: (0, 0)),
        ],
        out_specs=pl.BlockSpec((tm, LANES), lambda i: (i, 0)),
        out_shape=jax.ShapeDtypeStruct((n, LANES), F32),
        compiler_params=_cparams(("parallel",)),
        name="moe_router",
    )(x, w_bf, bias)


SUBLANES = 8


def _row_copy(src_hbm, dst, sem, src_row, group, sub):
    return pltpu.make_async_copy(src_hbm.at[pl.ds(src_row, 1)], dst.at[group, pl.ds(sub, 1)], sem)


def _gather_rows(idx_ref, src_hbm, dst, sem, n_rows):
    def body(g, carry):
        for sub in range(SUBLANES):
            _row_copy(src_hbm, dst, sem, idx_ref[0, 0, g * SUBLANES + sub], g, sub).start(priority=sub % 2)
        return carry
    lax.fori_loop(0, n_rows // SUBLANES, body, 0)


def _wait_rows(src_hbm, dst, sem, n_rows):
    def body(g, carry):
        for sub in range(SUBLANES):
            _row_copy(src_hbm, dst, sem, 0, g, sub).wait()
        return carry
    lax.fori_loop(0, n_rows // SUBLANES, body, 0)


def _expert_kernel(blk_e_ref, nvalid_ref, tok_ref, tok_next_ref, x_hbm, win_ref, wout_ref,
                   ys_ref, buf, sem):
    i = pl.program_id(0)
    nb = pl.num_programs(0)
    bm = buf.shape[1] * SUBLANES
    slot = i % 2
    nvalid = nvalid_ref[0]

    @pl.when(jnp.logical_and(i == 0, nvalid > 0))
    def _():
        _gather_rows(tok_ref, x_hbm, buf.at[0], sem.at[0], bm)

    @pl.when(jnp.logical_and(i + 1 < nb, i + 1 < nvalid))
    def _():
        nslot = (i + 1) % 2
        _gather_rows(tok_next_ref, x_hbm, buf.at[nslot], sem.at[nslot], bm)

    @pl.when(i < nvalid)
    def _():
        _wait_rows(x_hbm, buf.at[slot], sem.at[slot], bm)
        xb = buf[slot].reshape(bm, buf.shape[3]).astype(BF16)
        h = _dot(xb, win_ref[0])
        h1 = h[:, :D_EXPERT]
        act = h1 * _sigmoid(h1) * h[:, D_EXPERT:]
        ys_ref[...] = _dot(act.astype(BF16), wout_ref[0])

    @pl.when(i >= nvalid)
    def _():
        ys_ref[...] = jnp.zeros_like(ys_ref)


def _experts(x, row_tok, blk_e, nvalid, w_in_bf, w_out_bf, bm):
    n, d = x.shape
    nb = row_tok.shape[0]
    grid_spec = pltpu.PrefetchScalarGridSpec(
        num_scalar_prefetch=2,
        grid=(nb,),
        in_specs=[
            pl.BlockSpec((1, 1, bm), lambda i, be, nv: (i, 0, 0), memory_space=pltpu.SMEM),
            pl.BlockSpec((1, 1, bm), lambda i, be, nv: (jnp.minimum(i + 1, nb - 1), 0, 0),
                         memory_space=pltpu.SMEM),
            pl.BlockSpec(memory_space=pl.ANY),
            pl.BlockSpec((1, d, 2 * D_EXPERT), lambda i, be, nv: (be[i], 0, 0)),
            pl.BlockSpec((1, D_EXPERT, d), lambda i, be, nv: (be[i], 0, 0)),
        ],
        out_specs=pl.BlockSpec((bm, d), lambda i, be, nv: (i, 0)),
        scratch_shapes=[pltpu.VMEM((2, bm // SUBLANES, SUBLANES, d), F32), pltpu.SemaphoreType.DMA((2,))],
    )
    return pl.pallas_call(
        _expert_kernel,
        grid_spec=grid_spec,
        out_shape=jax.ShapeDtypeStruct((nb * bm, d), F32),
        compiler_params=_cparams(("arbitrary",)),
        name="moe_experts",
    )(blk_e, nvalid, row_tok, row_tok, x, w_in_bf, w_out_bf)


def _combine_kernel(pos_ref, pos_next_ref, ys_hbm, route_ref, x_ref, g_ref, b_ref, o_ref, buf, sem):
    i = pl.program_id(0)
    nb = pl.num_programs(0)
    tm = x_ref.shape[0]
    slot = i % 2

    @pl.when(i == 0)
    def _():
        _gather_rows(pos_ref, ys_hbm, buf.at[0], sem.at[0], 2 * tm)

    @pl.when(i + 1 < nb)
    def _():
        nslot = (i + 1) % 2
        _gather_rows(pos_next_ref, ys_hbm, buf.at[nslot], sem.at[nslot], 2 * tm)

    _wait_rows(ys_hbm, buf.at[slot], sem.at[slot], 2 * tm)
    route = route_ref[...]
    rows = buf[slot].reshape(2 * tm, buf.shape[3])
    y = rows[:tm] * route[:, 2:3] + rows[tm:] * route[:, 3:4]
    o_ref[...] = _layer_norm(ALPHA * x_ref[...] + y, g_ref[...], b_ref[...])


def _combine_ln(ys, pos_tab, route, x, g, b, tm):
    n, d = x.shape
    nb = n // tm
    return pl.pallas_call(
        _combine_kernel,
        grid=(nb,),
        in_specs=[
            pl.BlockSpec((1, 1, 2 * tm), lambda i: (i, 0, 0), memory_space=pltpu.SMEM),
            pl.BlockSpec((1, 1, 2 * tm), lambda i: (jnp.minimum(i + 1, nb - 1), 0, 0),
                         memory_space=pltpu.SMEM),
            pl.BlockSpec(memory_space=pl.ANY),
            pl.BlockSpec((tm, LANES), lambda i: (i, 0)),
            pl.BlockSpec((tm, d), lambda i: (i, 0)),
            pl.BlockSpec((1, d), lambda i: (0, 0)),
            pl.BlockSpec((1, d), lambda i: (0, 0)),
        ],
        out_specs=pl.BlockSpec((tm, d), lambda i: (i, 0)),
        out_shape=jax.ShapeDtypeStruct((n, d), F32),
        scratch_shapes=[pltpu.VMEM((2, 2 * tm // SUBLANES, SUBLANES, d), F32), pltpu.SemaphoreType.DMA((2,))],
        compiler_params=_cparams(("arbitrary",)),
        name="moe_combine_ln",
    )(pos_tab, pos_tab, ys, route, x, g, b)


def _moe_plan(route, bm, tm):
    n = route.shape[0]
    eid = route[:, :2].astype(jnp.int32).reshape(-1)
    na = eid.shape[0]
    onehot = (eid[:, None] == jnp.arange(N_EXPERTS, dtype=jnp.int32)[None, :]).astype(jnp.int32)
    csum = jnp.cumsum(onehot, axis=0)
    rank = jnp.sum(csum * onehot, axis=1) - 1
    counts = csum[-1]
    padded = (counts + bm - 1) // bm * bm
    pad_end = jnp.cumsum(padded)
    pad_off = pad_end - padded
    start = jnp.cumsum(counts) - counts
    dest = pad_off[eid] + rank
    nb = -(-na // bm) + N_EXPERTS
    order = jnp.argsort(eid, stable=True).astype(jnp.int32)
    blk_row0 = jnp.arange(nb, dtype=jnp.int32) * bm
    blk_e = jnp.minimum(jnp.sum((blk_row0[:, None] >= pad_end[None, :]).astype(jnp.int32), axis=1),
                        N_EXPERTS - 1)
    blk_local0 = blk_row0 - pad_off[blk_e]
    blk_left = jnp.where(blk_row0 < pad_end[-1], counts[blk_e] - blk_local0, 0)
    j = jnp.arange(bm, dtype=jnp.int32)
    src = jnp.clip((start[blk_e] + blk_local0)[:, None] + j[None, :], 0, na - 1)
    row_tok = jnp.where(j[None, :] < blk_left[:, None], order[src] // 2, 0).astype(jnp.int32).reshape(nb, 1, bm)
    nvalid = (pad_end[-1] // bm).astype(jnp.int32).reshape(1)
    pos_tab = dest.astype(jnp.int32).reshape(n // tm, tm, 2).transpose(0, 2, 1).reshape(n // tm, 1, 2 * tm)
    return row_tok, blk_e, nvalid, pos_tab


def _hier_moe_ln(x, w_router_bf, b_router, w_in_bf, w_out_bf, g, b):
    n = x.shape[0]
    tm_r = _pick_tile((512, 256, 128, 64), divides=(n,))
    tm_c = _pick_tile((256, 128, 64), divides=(n,))
    bm = 256 if (2 * n) % 256 == 0 else 128
    route = _router(x, w_router_bf, b_router, tm_r)
    row_tok, blk_e, nvalid, pos_tab = _moe_plan(route, bm, tm_c)
    ys = _experts(x, row_tok, blk_e, nvalid, w_in_bf, w_out_bf, bm)
    return _combine_ln(ys, pos_tab, route, x, g, b, tm_c)


def _head_sum(z, ones_blk):
    hi = z.astype(BF16)
    lo = (z - hi.astype(F32)).astype(BF16)
    return _dot(hi, ones_blk) + _dot(lo, ones_blk)


def _rwkv_proj_kernel(*refs, has_vres):
    if has_vres:
        (x_ref, xp_ref, mu_ref, wr_ref, wk_ref, wv_ref, w0_ref, w1_ref, w2_ref, a0_ref, a1_ref, a2_ref,
         g1_ref, g2_ref, kk_ref, ka_ref, ones_ref, v0_ref, v1_ref, v2_ref, vf_ref,
         r_o, k_o, v_o, kk_o, b_o, lw_o, g_o) = refs
    else:
        (x_ref, xp_ref, mu_ref, wr_ref, wk_ref, wv_ref, w0_ref, w1_ref, w2_ref, a0_ref, a1_ref, a2_ref,
         g1_ref, g2_ref, kk_ref, ka_ref, ones_ref,
         r_o, k_o, v_o, kk_o, b_o, lw_o, g_o) = refs
    x = x_ref[...]
    tm = x.shape[0]
    heads_of_chunks = jnp.concatenate(
        [jnp.broadcast_to(xp_ref[0, c:c + 1, :], (CHUNK, x.shape[1])) for c in range(tm // CHUNK)], axis=0)
    chunk_row = lax.broadcasted_iota(jnp.int32, (tm, 1), 0) & (CHUNK - 1)
    xx = jnp.where(chunk_row == 0, heads_of_chunks, pltpu.roll(x, 1, 0)) - x
    mu = mu_ref[...]

    def mix(j):
        return (x + xx * mu[j:j + 1]).astype(BF16)

    r = _dot(mix(0), wr_ref[...])
    z = w0_ref[...] + _dot(jnp.tanh(_dot(mix(1), w1_ref[...])).astype(BF16), w2_ref[...])
    wlog = -(jnp.maximum(-z, 0.0) + jnp.log(1.0 + jnp.exp(-jnp.abs(z)))) - 0.5
    lw = -jnp.exp(wlog)
    kraw = _dot(mix(2), wk_ref[...])
    xv = mix(3)
    v = _dot(xv, wv_ref[...])
    if has_vres:
        gate_v = _sigmoid(v0_ref[...] + _dot(_dot(xv, v1_ref[...]).astype(BF16), v2_ref[...]))
    a = _sigmoid(a0_ref[...] + _dot(_dot(mix(4), a1_ref[...]).astype(BF16), a2_ref[...]))
    g_o[...] = _dot(_sigmoid(_dot(mix(5), g1_ref[...])).astype(BF16), g2_ref[...])
    kkraw = kraw * kk_ref[...]
    kmod = kraw * (1.0 + (a - 1.0) * ka_ref[...])
    ones_blk = ones_ref[...]
    for p in range(r_o.shape[0]):
        sl = slice(p * LANES, (p + 1) * LANES)
        kkp = kkraw[:, sl]
        nrm = jnp.sqrt(_head_sum(kkp * kkp, ones_blk))
        kkn = kkp / jnp.maximum(nrm, 1e-12)
        vp = v[:, sl]
        if has_vres:
            vp = vp + (vf_ref[p] - vp) * gate_v[:, sl]
        r_o[p] = r[:, sl]
        k_o[p] = kmod[:, sl]
        v_o[p] = vp
        kk_o[p] = kkn
        b_o[p] = kkn * a[:, sl]
        lw_o[p] = lw[:, sl]


def _rwkv_proj(x, chunk_prev, prm, vfirst, tm):
    n, d = x.shape
    npair = d // LANES
    has_vres = vfirst is not None
    row = lambda i: (i, 0)
    const2 = lambda i: (0, 0)
    pair_spec = pl.BlockSpec((npair, tm, LANES), lambda i: (0, i, 0))

    def wspec(w):
        return pl.BlockSpec(w.shape, const2)

    args = [x, chunk_prev, prm["mu"], prm["wr"], prm["wk"], prm["wv"], prm["w0"], prm["w1"], prm["w2"],
            prm["a0"], prm["a1"], prm["a2"], prm["g1"], prm["g2"], prm["k_k"], prm["k_a"], prm["ones"]]
    in_specs = ([pl.BlockSpec((tm, d), row), pl.BlockSpec((1, tm // CHUNK, d), lambda i: (i, 0, 0))]
                + [wspec(w) for w in args[2:]])
    if has_vres:
        args += [prm["v0"], prm["v1"], prm["v2"], vfirst]
        in_specs += [wspec(prm["v0"]), wspec(prm["v1"]), wspec(prm["v2"]), pair_spec]
    pair_shape = jax.ShapeDtypeStruct((npair, n, LANES), F32)
    return pl.pallas_call(
        functools.partial(_rwkv_proj_kernel, has_vres=has_vres),
        grid=(n // tm,),
        in_specs=in_specs,
        out_specs=[pair_spec] * 6 + [pl.BlockSpec((tm, d), row)],
        out_shape=[pair_shape] * 6 + [jax.ShapeDtypeStruct((n, d), F32)],
        compiler_params=_cparams(("parallel",)),
        name="rwkv_proj",
    )(*args)


def _bdot(a, b, dims):
    return lax.dot_general(a.astype(BF16), b.astype(BF16), dims, preferred_element_type=F32)


_NN = (((2,), (1,)), ((0,), (0,)))
_NT = (((2,), (2,)), ((0,), (0,)))
_TN = (((1,), (1,)), ((0,), (0,)))


def _wkv_kernel(r_ref, k_ref, v_ref, kk_ref, b_ref, lw_ref, s0_ref, tri_ref, y_ref, sout_ref, s_scr,
                *, chunks_per_seq, prompt_chunks):
    step = pl.program_id(0)
    in_prompt = step < prompt_chunks
    cidx = step % chunks_per_seq
    first = jnp.logical_or(jnp.logical_not(in_prompt), cidx == 0)
    last = jnp.logical_or(jnp.logical_not(in_prompt), cidx == chunks_per_seq - 1)

    @pl.when(first)
    def _():
        s_scr[...] = s0_ref[0]

    npair = r_ref.shape[0]
    c = r_ref.shape[1]
    lw = lw_ref[...]
    tri = jnp.broadcast_to(tri_ref[...][None], (npair, c, c))
    lw_hi = lw.astype(BF16)
    lw_lo = (lw - lw_hi.astype(F32)).astype(BF16)
    cum = (lax.dot_general(tri, lw_hi, _NN, preferred_element_type=F32)
           + lax.dot_general(tri, lw_lo, _NN, preferred_element_type=F32))
    cum_end = cum[:, c - 1:c, :]
    e_inc = jnp.exp(cum)
    e_exc = jnp.exp(cum - lw)
    e_neg = jnp.exp(-cum)
    e_end = jnp.exp(cum_end - cum)
    p_end = jnp.exp(cum_end)

    lane = lax.broadcasted_iota(jnp.int32, (npair, c, LANES), 2)
    low = lane < HEAD_B

    def stack(z):
        return jnp.concatenate([jnp.where(low, z, 0.0), jnp.where(low, 0.0, z)], axis=1)

    kk = kk_ref[...]
    bb = b_ref[...]
    kmod = k_ref[...]
    a_t = stack(-kk * e_exc)
    r_t = stack(r_ref[...] * e_inc)
    b_t = stack(bb * e_neg)
    k_t = stack(kmod * e_neg)
    b_h = stack(bb * e_end)
    k_h = stack(kmod * e_end)
    v_s = stack(v_ref[...])

    ar = jnp.concatenate([a_t, r_t], axis=1)
    bk = jnp.concatenate([b_t, k_t], axis=1)
    gmat = _bdot(ar, bk, _NT)
    n2 = 2 * c
    ri = lax.broadcasted_iota(jnp.int32, (npair, n2, n2), 1) & (c - 1)
    ci = lax.broadcasted_iota(jnp.int32, (npair, n2, n2), 2) & (c - 1)
    strict = ri > ci
    incl = ri >= ci
    l_m = jnp.where(strict, gmat[:, :n2, :n2], 0.0)
    m_m = jnp.where(strict, gmat[:, :n2, n2:], 0.0)
    rb = jnp.where(incl, gmat[:, n2:, :n2], 0.0)
    rk = jnp.where(incl, gmat[:, n2:, n2:], 0.0)

    s_prev = s_scr[...]
    sa = _bdot(ar, s_prev, _NT)
    w_m = sa[:, :n2] + _bdot(m_m, v_s, _NN)

    eye = (lax.broadcasted_iota(jnp.int32, (npair, n2, n2), 1)
           == lax.broadcasted_iota(jnp.int32, (npair, n2, n2), 2)).astype(F32)
    t_m = eye + l_m
    pw = _bdot(l_m, l_m, _NN)
    span = 2
    while span < c:
        t_m = t_m + _bdot(t_m, pw, _NN)
        span *= 2
        if span < c:
            pw = _bdot(pw, pw, _NN)
    u_s = _bdot(t_m, w_m, _NN)

    y2 = sa[:, n2:] + _bdot(jnp.concatenate([rb, rk], axis=2), jnp.concatenate([u_s, v_s], axis=1), _NN)
    y_ref[...] = y2[:, :c] + y2[:, c:]

    s_new = s_prev * p_end + _bdot(jnp.concatenate([u_s, v_s], axis=1),
                                   jnp.concatenate([b_h, k_h], axis=1), _TN)
    s_scr[...] = s_new

    @pl.when(last)
    def _():
        sout_ref[0] = s_new


def _wkv(r, k, v, kk, b, lw, s0_pairs, n_prompt, seq):
    npair, n, _ = r.shape
    c = CHUNK
    chunks_per_seq = seq // c
    prompt_chunks = n_prompt // c
    nseq = s0_pairs.shape[0]
    n_prompt_seq = n_prompt // seq
    tri = jnp.tril(jnp.ones((c, c), F32)).astype(BF16)

    def seq_map(s):
        return (jnp.where(s < prompt_chunks, s // chunks_per_seq, n_prompt_seq + s - prompt_chunks), 0, 0, 0)

    tok_spec = pl.BlockSpec((npair, c, LANES), lambda s: (0, s, 0))
    state_spec = pl.BlockSpec((1, npair, LANES, LANES), seq_map)
    return pl.pallas_call(
        functools.partial(_wkv_kernel, chunks_per_seq=chunks_per_seq, prompt_chunks=prompt_chunks),
        grid=(n // c,),
        in_specs=[tok_spec] * 6 + [state_spec, pl.BlockSpec((c, c), lambda s: (0, 0))],
        out_specs=[tok_spec, state_spec],
        out_shape=[jax.ShapeDtypeStruct((npair, n, LANES), F32),
                   jax.ShapeDtypeStruct((nseq, npair, LANES, LANES), F32)],
        scratch_shapes=[pltpu.VMEM((npair, LANES, LANES), F32)],
        compiler_params=_cparams(("arbitrary",)),
        name="wkv_chunked",
    )(r, k, v, kk, b, lw, s0_pairs, tri)


def _state_to_pairs(s):
    nseq, h, hv, hk = s.shape
    s = s.reshape(nseq, h // 2, 2, hv, hk)
    z = jnp.zeros_like(s[:, :, 0])
    top = jnp.concatenate([s[:, :, 0], z], axis=-1)
    bot = jnp.concatenate([z, s[:, :, 1]], axis=-1)
    return jnp.concatenate([top, bot], axis=-2)


def _pairs_to_state(sp):
    nseq, npair = sp.shape[:2]
    s0 = sp[:, :, :HEAD_B, :HEAD_B]
    s1 = sp[:, :, HEAD_B:, HEAD_B:]
    return jnp.stack([s0, s1], axis=2).reshape(nseq, 2 * npair, HEAD_B, HEAD_B)


def _rwkv_out_kernel(y_ref, r_ref, k_ref, v_ref, g_ref, x_ref, lg_ref, lb_ref, rk_ref, ones_ref,
                     wo_ref, ng_ref, nb_ref, o_ref):
    ones_blk = ones_ref[...]
    inv = 1.0 / HEAD_B
    parts = []
    for p in range(y_ref.shape[0]):
        sl = slice(p * LANES, (p + 1) * LANES)
        y = y_ref[p]
        mean = _head_sum(y, ones_blk) * inv
        yc = y - mean
        var = _head_sum(yc * yc, ones_blk) * inv
        yn = yc * lax.rsqrt(var + GN_EPS) * lg_ref[:, sl] + lb_ref[:, sl]
        bonus = _head_sum(r_ref[p] * k_ref[p] * rk_ref[:, sl], ones_blk) * v_ref[p]
        parts.append(((yn + bonus) * g_ref[:, sl]).astype(BF16))
    z = jnp.concatenate(parts, axis=1)
    h = _dot(z, wo_ref[...])
    o_ref[...] = _layer_norm(ALPHA * x_ref[...] + h, ng_ref[...], nb_ref[...])


def _rwkv_out(y, r, k, v, g, x, prm, ln_g, ln_b, tm):
    n, d = x.shape
    npair = d // LANES
    row = lambda i: (i, 0)
    const2 = lambda i: (0, 0)
    pair_spec = pl.BlockSpec((npair, tm, LANES), lambda i: (0, i, 0))
    vec = pl.BlockSpec((1, d), const2)
    return pl.pallas_call(
        _rwkv_out_kernel,
        grid=(n // tm,),
        in_specs=[pair_spec] * 4 + [pl.BlockSpec((tm, d), row), pl.BlockSpec((tm, d), row),
                                    vec, vec, vec, pl.BlockSpec((LANES, LANES), const2),
                                    pl.BlockSpec((d, d), const2), vec, vec],
        out_specs=pl.BlockSpec((tm, d), row),
        out_shape=jax.ShapeDtypeStruct((n, d), F32),
        compiler_params=_cparams(("parallel",)),
        name="rwkv_out",
    )(y, r, k, v, g, x, prm["lnx_g"], prm["lnx_b"], prm["r_k"], prm["ones"], prm["wo"], ln_g, ln_b)


def _pad_cols(w, mult):
    pad = (-w.shape[1]) % mult
    return jnp.pad(w, ((0, 0), (0, pad))) if pad else w


def _pad_rows(w, mult):
    pad = (-w.shape[0]) % mult
    return jnp.pad(w, ((0, pad), (0, 0))) if pad else w


def _rwkv_params(j, mu_b, w_rkv_b, w0_b, w1_b, w2_b, a0_b, a1_b, a2_b, v0_b, v1_b, v2_b, g1_b, g2_b,
                 k_k_b, k_a_b, r_k_b, lnx_g_b, lnx_b_b, w_o_b):
    d = mu_b.shape[-1]
    lane = jnp.arange(LANES)
    ones = (lane[:, None] // HEAD_B == lane[None, :] // HEAD_B).astype(BF16)
    prm = dict(
        mu=mu_b[j], wr=w_rkv_b[j, 0].astype(BF16), wk=w_rkv_b[j, 1].astype(BF16), wv=w_rkv_b[j, 2].astype(BF16),
        w0=w0_b[j].reshape(1, d), w1=_pad_cols(w1_b[j], LANES).astype(BF16), w2=_pad_rows(w2_b[j], LANES).astype(BF16),
        a0=a0_b[j].reshape(1, d), a1=_pad_cols(a1_b[j], LANES).astype(BF16), a2=_pad_rows(a2_b[j], LANES).astype(BF16),
        g1=_pad_cols(g1_b[j], LANES).astype(BF16), g2=_pad_rows(g2_b[j], LANES).astype(BF16),
        k_k=k_k_b[j].reshape(1, d), k_a=k_a_b[j].reshape(1, d), r_k=r_k_b[j].reshape(1, d),
        lnx_g=lnx_g_b[j].reshape(1, d), lnx_b=lnx_b_b[j].reshape(1, d), wo=w_o_b[j].astype(BF16), ones=ones,
    )
    if j > 0:
        prm.update(v0=v0_b[j - 1].reshape(1, d), v1=_pad_cols(v1_b[j - 1], LANES).astype(BF16),
                   v2=_pad_rows(v2_b[j - 1], LANES).astype(BF16))
    return prm


def kernel(x_prompt, x_sample, cache_k_a, cache_v_a, state_shift_b, state_wkv_b, w_qkv_a, sinks_a, w_o_a, mu_b, w_rkv_b, w0_b, w1_b, w2_b, a0_b, a1_b, a2_b, v0_b, v1_b, v2_b, g1_b, g2_b, k_k_b, k_a_b, r_k_b, lnx_g_b, lnx_b_b, w_o_b, ln_g, ln_b, w_rg, b_rg, w_re, b_re, w_exp_in, w_exp_out):
    bp, seq, d = x_prompt.shape
    bs, dec_seq, _ = x_sample.shape
    n_p, n_s = bp * seq, bs * dec_seq
    n = n_p + n_s
    nk = N_KV_HEADS * HEAD_DIM
    depth = ln_g.shape[0]

    x = jnp.concatenate([x_prompt.reshape(n_p, d), x_sample.reshape(n_s, d)], axis=0)

    tm_qkv = _pick_tile((512, 256, 128, 64), divides=(seq, n_s), multiple_of=(dec_seq,))
    tm_ln = _pick_tile((512, 256, 128, 64), divides=(n,))
    tm_rw = _pick_tile((256, 128, 64), divides=(n,))
    qb = _pick_tile((256, 128), divides=(seq,))
    cos_t, sin_t = _rope_tables(seq, dec_seq, tm_qkv)

    kp_l, vp_l, ks_l, vs_l = [], [], [], []
    shp_l, stp_l, shs_l, sts_l = [], [], [], []
    vfirst = None
    for i in range(depth):
        j = i // 2
        if i % 2 == 0:
            q, k, v = _qkv_rope(x, w_qkv_a[j].astype(BF16), cos_t, sin_t, tm_qkv, n_p, seq)
            op = _attention(q, k, k, v, v, sinks_a[j], bp, seq, qb, 0, True, True)
            os_ = _attention(q, cache_k_a[j].reshape(bs * WINDOW_ROWS, nk), k,
                             cache_v_a[j].reshape(bs * WINDOW_ROWS, nk), v, sinks_a[j],
                             bs, dec_seq, dec_seq, n_p, False, False)
            x = _proj_ln(op, os_, w_o_a[j].astype(BF16), x, ln_g[i, 0].reshape(1, d),
                         ln_b[i, 0].reshape(1, d), tm_ln)
            keep = min(WINDOW_ROWS, seq)
            kp_l.append(k[:n_p].reshape(bp, seq, nk)[:, seq - keep:].reshape(bp, keep, N_KV_HEADS, HEAD_DIM))
            vp_l.append(v[:n_p].reshape(bp, seq, nk)[:, seq - keep:].reshape(bp, keep, N_KV_HEADS, HEAD_DIM))
            ks_l.append(k[n_p:].reshape(bs, dec_seq, N_KV_HEADS, HEAD_DIM))
            vs_l.append(v[n_p:].reshape(bs, dec_seq, N_KV_HEADS, HEAD_DIM))
        else:
            prm = _rwkv_params(j, mu_b, w_rkv_b, w0_b, w1_b, w2_b, a0_b, a1_b, a2_b, v0_b, v1_b, v2_b,
                               g1_b, g2_b, k_k_b, k_a_b, r_k_b, lnx_g_b, lnx_b_b, w_o_b)
            chunk_last = x.reshape(n // CHUNK, CHUNK, d)[:, CHUNK - 1]
            prev_p = jnp.concatenate([jnp.zeros((1, d), F32), chunk_last[:n_p // CHUNK - 1]], axis=0)
            prev_p = jnp.where((jnp.arange(n_p // CHUNK) % (seq // CHUNK) == 0)[:, None], 0.0, prev_p)
            prev_s = jnp.where((jnp.arange(n_s // CHUNK) % (dec_seq // CHUNK) == 0)[:, None],
                               jnp.repeat(state_shift_b[j], dec_seq // CHUNK, axis=0),
                               chunk_last[n_p // CHUNK - 1:n // CHUNK - 1])
            chunk_prev = jnp.concatenate([prev_p, prev_s], axis=0).reshape(n // tm_rw, tm_rw // CHUNK, d)
            r, k, v, kk, b, lw, g = _rwkv_proj(x, chunk_prev, prm, vfirst, tm_rw)
            if vfirst is None:
                vfirst = v
            s0 = jnp.concatenate([jnp.zeros((bp,) + state_wkv_b.shape[2:], F32), state_wkv_b[j]], axis=0)
            y, s_fin = _wkv(r, k, v, kk, b, lw, _state_to_pairs(s0), n_p, seq)
            s_fin = _pairs_to_state(s_fin)
            seq_last = chunk_last.reshape(-1, 1, d)
            shp_l.append(seq_last[seq // CHUNK - 1:n_p // CHUNK:seq // CHUNK, 0])
            shs_l.append(seq_last[n_p // CHUNK + dec_seq // CHUNK - 1::dec_seq // CHUNK, 0])
            stp_l.append(s_fin[:bp])
            sts_l.append(s_fin[bp:])
            x = _rwkv_out(y, r, k, v, g, x, prm, ln_g[i, 0].reshape(1, d), ln_b[i, 0].reshape(1, d), tm_rw)
        w_router = jnp.pad(jnp.concatenate([w_re[i], w_rg[i]], axis=1),
                           ((0, 0), (0, LANES - N_EXPERTS - N_GROUPS))).astype(BF16)
        b_router = jnp.pad(jnp.concatenate([b_re[i], b_rg[i]]), (0, LANES - N_EXPERTS - N_GROUPS)).reshape(1, LANES)
        x = _hier_moe_ln(x, w_router, b_router, w_exp_in[i].astype(BF16), w_exp_out[i].astype(BF16),
                         ln_g[i, 1].reshape(1, d), ln_b[i, 1].reshape(1, d))

    return (x[:n_p].reshape(bp, seq, d), x[n_p:].reshape(bs, dec_seq, d),
            jnp.stack(kp_l), jnp.stack(vp_l), jnp.stack(shp_l), jnp.stack(stp_l),
            jnp.stack(ks_l), jnp.stack(vs_l), jnp.stack(shs_l), jnp.stack(sts_l))
```
